```python
import jax
import jax.numpy as jnp
from jax import lax
import numpy as np

D_MODEL = 1024
BATCH = 4
SEQ = 4096
DEPTH = 2
DEC_BATCH = 128
DEC_SEQ = 1
PAST_LEN = 16384
PAGE_SIZE = 128

HEAD_DIM = 64
ROT_DIM = HEAD_DIM // 4
ROPE_THETA = 500000.0
QBLK = 128
EPS = 1e-6
DN_HEADS = 4
DN_DK = 128
DN_DV = 128
DN_CONV = 4
DN_CHUNK = 64
DN_QKV = DN_HEADS * (2 * DN_DK + DN_DV)
SWA_HEADS = 8
SWA_KV_HEADS = 2
SWA_WINDOW = 128
L0_IN = DN_QKV + DN_HEADS * DN_DV + 2 * DN_HEADS + (SWA_HEADS + 2 * SWA_KV_HEADS) * HEAD_DIM
L0_MIX = DN_HEADS * DN_DV + SWA_HEADS * HEAD_DIM
NSA_HEADS = 16
NSA_KV_HEADS = 4
CMP_BLK = 32
CMP_STRIDE = 16
SEL_BLK = 64
N_SEL = 16
NSA_WINDOW = 512
SEL_BONUS = 1e4
L1_IN = (NSA_HEADS + 6 * NSA_KV_HEADS) * HEAD_DIM + 3 * NSA_HEADS
L1_MIX = NSA_HEADS * HEAD_DIM
MEM_TOKENS = 256
MEM_HEADS = 4
MEM_HEAD_DIM = 128
MEM_WIDTH = MEM_HEADS * MEM_HEAD_DIM
D_FF = 2816
FFN_CONV = 3

kernel_name = 'hybrid_deltanet_swa_nsa_decoder_step'


def rmsnorm(x, g):
    xf = x.astype(jnp.float32)
    y = xf * lax.rsqrt(jnp.mean(xf * xf, axis=-1, keepdims=True) + EPS)
    return (y * g.astype(jnp.float32)).astype(x.dtype)


def l2norm(x):
    xf = x.astype(jnp.float32)
    return xf * lax.rsqrt(jnp.sum(xf * xf, axis=-1, keepdims=True) + EPS)


def rope_partial(x, pos):
    half = ROT_DIM // 2
    inv = ROPE_THETA ** (-jnp.arange(half, dtype=jnp.float32) / half)
    ang = pos.astype(jnp.float32)[:, None] * inv[None, :]
    cos, sin = jnp.cos(ang)[:, None, :], jnp.sin(ang)[:, None, :]
    xf = x.astype(jnp.float32)
    x1, x2, rest = xf[..., :half], xf[..., half:ROT_DIM], xf[..., ROT_DIM:]
    out = jnp.concatenate([x1 * cos - x2 * sin, x1 * sin + x2 * cos, rest], axis=-1)
    return out.astype(x.dtype)


def masked_softmax(s, mask):
    s = jnp.where(mask, s.astype(jnp.float32), -jnp.inf)
    m = jnp.max(s, axis=-1, keepdims=True)
    m = jnp.where(jnp.isfinite(m), m, 0.0)
    e = jnp.exp(s - m)
    return e / jnp.maximum(jnp.sum(e, axis=-1, keepdims=True), 1e-30)


def sink_softmax(s, mask, sink):
    s = jnp.where(mask, s.astype(jnp.float32), -jnp.inf)
    sink = sink.astype(jnp.float32)
    m = jnp.maximum(jnp.max(s, axis=-1, keepdims=True), sink)
    e = jnp.exp(s - m)
    return e / (jnp.sum(e, axis=-1, keepdims=True) + jnp.exp(sink - m))


def causal_dwconv(u, buf, w, b=None):
    T, K = u.shape[1], w.shape[0]
    ext = jnp.concatenate([buf.astype(u.dtype), u], axis=1)
    out = ext[:, 0:T] * w[0]
    for j in range(1, K):
        out = out + ext[:, j:j + T] * w[j]
    if b is not None:
        out = out + b
    return out, ext[:, T:]


def window_attention(q, q_pos0, k, v, k_pos0, window, sinks):
    B, Tq, H, hd = q.shape
    Tk, G = k.shape[1], k.shape[2]
    R = H // G
    qb = min(QBLK, Tq)
    nqb = -(-Tq // qb)
    Tqp = nqb * qb
    off = q_pos0 - k_pos0
    span = window + qb
    pad_end = max(0, off + Tqp - Tk)
    qp = jnp.pad(q, ((0, 0), (0, Tqp - Tq), (0, 0), (0, 0))).reshape(B, nqb, qb, G, R, hd)
    kp = jnp.pad(k, ((0, 0), (window, pad_end), (0, 0), (0, 0)))
    vp = jnp.pad(v, ((0, 0), (window, pad_end), (0, 0), (0, 0)))
    idx = off + np.arange(nqb)[:, None] * qb + np.arange(span)[None, :]
    kb, vb = kp[:, idx], vp[:, idx]
    qpos = q_pos0 + np.arange(Tqp).reshape(nqb, qb)
    kpos = idx - window + k_pos0
    kval = (idx >= window) & (idx < window + Tk)
    dpos = qpos[:, :, None] - kpos[:, None, :]
    mask = kval[:, None, :] & (dpos >= 0) & (dpos < window)
    mask = jnp.asarray(mask)[None, :, None, None]
    s = jnp.einsum('bnqgrd,bnkgd->bngrqk', qp, kb) * (hd ** -0.5)
    if sinks is None:
        p = masked_softmax(s, mask)
    else:
        p = sink_softmax(s, mask, sinks.reshape(G, R, 1, 1))
    o = jnp.einsum('bngrqk,bnkgd->bnqgrd', p, vb.astype(jnp.float32))
    return o.reshape(B, Tqp, H, hd)[:, :Tq]


def gated_delta_chunked(q, k, v, g, beta, S0):
    B, T, H, dk = q.shape
    dv = v.shape[-1]
    C = min(DN_CHUNK, T)
    n = -(-T // C)
    pad = n * C - T

    def to_chunks(a):
        a = jnp.pad(a.astype(jnp.float32), [(0, 0), (0, pad)] + [(0, 0)] * (a.ndim - 2))
        a = a.reshape((B, n, C) + a.shape[2:])
        return jnp.moveaxis(a, 3, 1)

    q, k, v, g, beta = (to_chunks(a) for a in (q, k, v, g, beta))
    gc = jnp.cumsum(g, axis=-1)
    ii = jnp.arange(C)[:, None]
    jj = jnp.arange(C)[None, :]
    decay = jnp.exp(jnp.where(ii >= jj, gc[..., :, None] - gc[..., None, :], -jnp.inf))
    kb = k * beta[..., None]
    m_strict = jnp.where(ii > jj, jnp.einsum('bhnid,bhnjd->bhnij', kb, k) * decay, 0.0)
    a_mat = m_strict + jnp.eye(C, dtype=jnp.float32)
    w = lax.linalg.triangular_solve(a_mat, kb * jnp.exp(gc)[..., None], left_side=True, lower=True, unit_diagonal=True)
    u = lax.linalg.triangular_solve(a_mat, v * beta[..., None], left_side=True, lower=True, unit_diagonal=True)
    qk = jnp.einsum('bhnid,bhnjd->bhnij', q, k) * decay

    def step(S, xs):
        q_c, k_c, u_c, w_c, g_c, qk_c = xs
        v_new = u_c - jnp.einsum('bhck,bhkv->bhcv', w_c, S)
        o = (jnp.einsum('bhck,bhkv->bhcv', q_c * jnp.exp(g_c)[..., None], S)
             + jnp.einsum('bhij,bhjv->bhiv', qk_c, v_new))
        g_last = g_c[..., -1]
        S = (S * jnp.exp(g_last)[..., None, None]
             + jnp.einsum('bhck,bhcv->bhkv', k_c * jnp.exp(g_last[..., None] - g_c)[..., None], v_new))
        return S, o

    xs = tuple(jnp.moveaxis(a, 2, 0) for a in (q, k, u, w, gc, qk))
    S, o = lax.scan(step, S0.astype(jnp.float32), xs)
    o = jnp.moveaxis(jnp.moveaxis(o, 0, 2), 1, 3).reshape(B, n * C, H, dv)[:, :T]
    return o, S


def compress_blocks(rows, pe, w1, w2):
    n, blk, g, hd = rows.shape
    h = (rows + pe[None, :, None, :]).transpose(0, 2, 1, 3).reshape(n, g, blk * hd)
    return jax.nn.silu(h @ w1) @ w2


def nsa_cmp_sel(q_nope, q_rope, kc, vc, ks, vs, q_off, P):
    Tq, L = q_nope.shape[0], kc.shape[0]
    G, R, hd = NSA_KV_HEADS, NSA_HEADS // NSA_KV_HEADS, HEAD_DIM
    scale = hd ** -0.5
    n_cmp = (L - CMP_BLK) // CMP_STRIDE + 1
    cidx = np.arange(n_cmp)[:, None] * CMP_STRIDE + np.arange(CMP_BLK)[None, :]
    k_cmp = rmsnorm(compress_blocks(kc[cidx], P['l1_cmp_pe_k'], P['l1_cmp_w1_k'], P['l1_cmp_w2_k']), P['l1_k_norm_cmp'])
    v_cmp = compress_blocks(vc[cidx], P['l1_cmp_pe_v'], P['l1_cmp_w1_v'], P['l1_cmp_w2_v']).astype(jnp.float32)
    cmp_end = jnp.asarray(np.arange(n_cmp) * CMP_STRIDE + CMP_BLK - 1, jnp.int32)
    n_sel = -(-L // SEL_BLK)
    pad_l = n_sel * SEL_BLK - L
    ks_blk = jnp.pad(ks, ((0, pad_l), (0, 0), (0, 0))).reshape(n_sel, SEL_BLK, G, hd).transpose(2, 0, 1, 3)
    vs_blk = jnp.pad(vs, ((0, pad_l), (0, 0), (0, 0))).reshape(n_sel, SEL_BLK, G, hd).transpose(2, 0, 1, 3)
    cstart = np.arange(n_cmp)[:, None] * CMP_STRIDE
    sstart = np.arange(n_sel)[None, :] * SEL_BLK
    cover = jnp.asarray(((cstart < sstart + SEL_BLK) & (cstart + CMP_BLK > sstart)).astype(np.float32))
    k_top = min(N_SEL, n_sel)
    qb = min(QBLK, Tq)
    nqb = -(-Tq // qb)
    padq = nqb * qb - Tq

    def blocks(a):
        return jnp.pad(a, ((0, padq), (0, 0), (0, 0))).reshape(nqb, qb, G, R, hd)

    p0s = jnp.asarray(q_off + np.arange(nqb) * qb, jnp.int32)
    sel_ids = jnp.arange(n_sel, dtype=jnp.int32)
    g_ix = jnp.arange(G)[None, :, None]

    def one_block(a):
        qn, qr, p0 = a
        t = p0 + jnp.arange(qb, dtype=jnp.int32)
        s_c = jnp.einsum('qgrd,cgd->qgrc', qn, k_cmp) * scale
        p_c = masked_softmax(s_c, (cmp_end[None, :] <= t[:, None])[:, None, None, :])
        o_c = jnp.einsum('qgrc,cgd->qgrd', p_c, v_cmp)
        imp = jnp.einsum('qgc,cs->qgs', p_c.sum(axis=2), cover)
        bt = (t // SEL_BLK)[:, None]
        forced = (sel_ids == 0) | (sel_ids == bt) | (sel_ids == bt - 1)
        valid = sel_ids * SEL_BLK <= t[:, None]
        imp = jnp.where(forced[:, None], SEL_BONUS, jnp.where(valid[:, None], imp, -SEL_BONUS))
        _, idx = lax.top_k(imp, k_top)
        kg = ks_blk[g_ix, idx].reshape(qb, G, k_top * SEL_BLK, hd)
        vg = vs_blk[g_ix, idx].reshape(qb, G, k_top * SEL_BLK, hd)
        kpos = (idx[..., None] * SEL_BLK + jnp.arange(SEL_BLK, dtype=jnp.int32)).reshape(qb, G, k_top * SEL_BLK)
        s_s = jnp.einsum('qgrd,qgkd->qgrk', qr, kg) * scale
        p_s = masked_softmax(s_s, (kpos <= t[:, None, None])[:, :, None, :])
        o_s = jnp.einsum('qgrk,qgkd->qgrd', p_s, vg.astype(jnp.float32))
        return o_c, o_s

    o_c, o_s = lax.map(one_block, (blocks(q_nope), blocks(q_rope), p0s))
    return (o_c.reshape(nqb * qb, G * R, hd)[:Tq], o_s.reshape(nqb * qb, G * R, hd)[:Tq])


def mixer_ab(h, pos0, S0, conv_buf, kbuf, vbuf, P):
    B, T, _ = h.shape
    pos = pos0 + jnp.arange(T, dtype=jnp.int32)
    u = h @ P['l0_w_in']
    sizes = [DN_QKV, DN_HEADS * DN_DV, DN_HEADS, DN_HEADS, SWA_HEADS * HEAD_DIM, SWA_KV_HEADS * HEAD_DIM]
    qkv, z, b_raw, a_raw, q_b, k_b, v_b = jnp.split(u, [int(c) for c in np.cumsum(sizes)], axis=-1)
    c, new_conv = causal_dwconv(qkv, conv_buf, P['l0_dn_conv_w'])
    c = jax.nn.silu(c)
    q, k, v = jnp.split(c, [DN_HEADS * DN_DK, 2 * DN_HEADS * DN_DK], axis=-1)
    q = l2norm(q.reshape(B, T, DN_HEADS, DN_DK)) * (DN_DK ** -0.5)
    k = l2norm(k.reshape(B, T, DN_HEADS, DN_DK))
    v = v.reshape(B, T, DN_HEADS, DN_DV).astype(jnp.float32)
    beta = jax.nn.sigmoid(b_raw.astype(jnp.float32))
    g = -jnp.exp(P['l0_dn_a_log'].astype(jnp.float32)) * jax.nn.softplus(a_raw.astype(jnp.float32) + P['l0_dn_dt_bias'].astype(jnp.float32))
    o_dn, S = gated_delta_chunked(q, k, v, g, beta, S0)
    o_dn = rmsnorm(o_dn, P['l0_dn_norm']) * jax.nn.silu(z.reshape(B, T, DN_HEADS, DN_DV).astype(jnp.float32))
    qs = rope_partial(rmsnorm(q_b.reshape(B, T, SWA_HEADS, HEAD_DIM), P['l0_swa_q_norm']), pos)
    ks = rope_partial(rmsnorm(k_b.reshape(B, T, SWA_KV_HEADS, HEAD_DIM), P['l0_swa_k_norm']), pos)
    vs = v_b.reshape(B, T, SWA_KV_HEADS, HEAD_DIM)
    k_all = jnp.concatenate([kbuf.astype(ks.dtype), ks], axis=1)
    v_all = jnp.concatenate([vbuf.astype(vs.dtype), vs], axis=1)
    o_sw = window_attention(qs, pos0, k_all, v_all, pos0 - kbuf.shape[1], SWA_WINDOW, P['l0_swa_sinks'])
    mix = jnp.concatenate([o_dn.reshape(B, T, -1), o_sw.reshape(B, T, -1)], axis=-1).astype(h.dtype)
    nb = min(SWA_WINDOW, k_all.shape[1])
    return mix @ P['l0_w_out'], (S.astype(S0.dtype), new_conv, k_all[:, -nb:], v_all[:, -nb:])


def mixer_nsa(h, pos0, kbuf, vbuf, past_fn, P):
    B, T, _ = h.shape
    G, H, hd = NSA_KV_HEADS, NSA_HEADS, HEAD_DIM
    pos = pos0 + jnp.arange(T, dtype=jnp.int32)
    u = h @ P['l1_w_in']
    sizes = [H * hd] + [G * hd] * 6
    q, kc, vc, ks, vs, kw, vw, gr = jnp.split(u, [int(c) for c in np.cumsum(sizes)], axis=-1)
    q_nope = rmsnorm(q.reshape(B, T, H, hd), P['l1_q_norm'])
    q_rope = rope_partial(q_nope, pos)
    kc = kc.reshape(B, T, G, hd)
    vc = vc.reshape(B, T, G, hd)
    ks = rope_partial(rmsnorm(ks.reshape(B, T, G, hd), P['l1_k_norm_sel']), pos)
    vs = vs.reshape(B, T, G, hd)
    kw = rope_partial(rmsnorm(kw.reshape(B, T, G, hd), P['l1_k_norm_win']), pos)
    vw = vw.reshape(B, T, G, hd)
    gates = jax.nn.sigmoid(gr.astype(jnp.float32)).reshape(B, T, H, 3)
    o_c, o_s = past_fn(q_nope, q_rope, kc, vc, ks, vs)
    kw_all = jnp.concatenate([kbuf.astype(kw.dtype), kw], axis=1)
    vw_all = jnp.concatenate([vbuf.astype(vw.dtype), vw], axis=1)
    o_w = window_attention(q_rope, pos0, kw_all, vw_all, pos0 - kbuf.shape[1], NSA_WINDOW, None)
    o = gates[..., 0:1] * o_c + gates[..., 1:2] * o_s + gates[..., 2:3] * o_w
    out = o.reshape(B, T, H * hd).astype(h.dtype) @ P['l1_w_out']
    nb = min(NSA_WINDOW, kw_all.shape[1])
    return out, (kc, vc, ks, vs, kw_all[:, -nb:], vw_all[:, -nb:])


def mem_kv(mem, layer, P):
    B, M, _ = mem.shape
    mn = rmsnorm(mem, P['mem_norm_kv'][layer])
    k = rmsnorm((mn @ P['mem_w_k'][layer]).reshape(B, M, MEM_HEADS, MEM_HEAD_DIM), P['mem_k_norm'][layer])
    v = (mn @ P['mem_w_v'][layer]).reshape(B, M, MEM_HEADS, MEM_HEAD_DIM)
    return k, v


def mem_attend(h, k, v, layer, P):
    B, T, _ = h.shape
    q = rmsnorm((h @ P['mem_w_q'][layer]).reshape(B, T, MEM_HEADS, MEM_HEAD_DIM), P['mem_q_norm'][layer])
    s = jnp.einsum('bthd,bmhd->bhtm', q, k.astype(q.dtype)).astype(jnp.float32) * (MEM_HEAD_DIM ** -0.5)
    p = jax.nn.softmax(s, axis=-1)
    o = jnp.einsum('bhtm,bmhd->bthd', p, v.astype(jnp.float32)).reshape(B, T, MEM_WIDTH).astype(h.dtype)
    return o @ P['mem_w_o'][layer]


def conv_ffn(h, buf, layer, P):
    u = h @ P['ffn_w_in'][layer]
    c, new_buf = causal_dwconv(u, buf, P['ffn_conv_w'][layer], P['ffn_conv_b'][layer])
    a, b = jnp.split(c, 2, axis=-1)
    return (jax.nn.silu(a) * b) @ P['ffn_w_out'][layer], new_buf


def trunk(x, pos0, dn_S, dn_conv, swa_k, swa_v, win_k, win_v, mem_k, mem_v, ffn_buf, past_fn, P):
    ffn_new = []
    for layer in range(DEPTH):
        h = rmsnorm(x, P['mix_norm'][layer])
        if layer % 2 == 0:
            o, ab_state = mixer_ab(h, pos0, dn_S, dn_conv, swa_k, swa_v, P)
        else:
            o, nsa_state = mixer_nsa(h, pos0, win_k, win_v, past_fn, P)
        x = x + o
        x = x + mem_attend(rmsnorm(x, P['mem_norm_x'][layer]), mem_k[layer], mem_v[layer], layer, P)
        f, fb = conv_ffn(rmsnorm(x, P['ffn_norm'][layer]), ffn_buf[layer], layer, P)
        x = x + f
        ffn_new.append(fb)
    return x, ab_state, nsa_state, jnp.stack(ffn_new)


def setup_inputs(seed: int = 0) -> dict:
    key = jax.random.key(seed)
    keys = iter(jax.random.split(key, 80))
    f32 = jnp.float32

    def nrm(shape, scale):
        return jax.random.normal(next(keys), shape, f32) * scale

    def gain(shape):
        return 1.0 + nrm(shape, 0.05)

    n_pages = PAST_LEN // PAGE_SIZE
    n_used = DEC_BATCH * n_pages
    n_pool = n_used + max(1, n_used // 4)
    swa_buf = min(SWA_WINDOW, PAST_LEN)
    win_buf = min(NSA_WINDOW, PAST_LEN)
    G, hd = NSA_KV_HEADS, HEAD_DIM
    perm = jax.random.permutation(next(keys), n_pool)
    page_table = perm[:n_used].reshape(DEC_BATCH, n_pages).astype(jnp.int32)
    return {
        'x_prompt': nrm((BATCH, SEQ, D_MODEL), 1.0),
        'x_sample': nrm((DEC_BATCH, DEC_SEQ, D_MODEL), 1.0),
        'state_dn': nrm((DEC_BATCH, DN_HEADS, DN_DK, DN_DV), 0.1),
        'state_dn_conv': nrm((DEC_BATCH, DN_CONV - 1, DN_QKV), 1.0),
        'cache_swa_k': nrm((DEC_BATCH, swa_buf, SWA_KV_HEADS, HEAD_DIM), 1.0),
        'cache_swa_v': nrm((DEC_BATCH, swa_buf, SWA_KV_HEADS, HEAD_DIM), 1.0),
        'cache_cmp_k': nrm((n_pool, PAGE_SIZE, G, hd), 1.0),
        'cache_cmp_v': nrm((n_pool, PAGE_SIZE, G, hd), 1.0),
        'cache_sel_k': nrm((n_pool, PAGE_SIZE, G, hd), 1.0),
        'cache_sel_v': nrm((n_pool, PAGE_SIZE, G, hd), 1.0),
        'cache_win_k': nrm((DEC_BATCH, win_buf, G, hd), 1.0),
        'cache_win_v': nrm((DEC_BATCH, win_buf, G, hd), 1.0),
        'cache_mem_k': nrm((DEPTH, DEC_BATCH, MEM_TOKENS, MEM_HEADS, MEM_HEAD_DIM), 1.0),
        'cache_mem_v': nrm((DEPTH, DEC_BATCH, MEM_TOKENS, MEM_HEADS, MEM_HEAD_DIM), 1.0),
        'state_ffn_conv': nrm((DEPTH, DEC_BATCH, FFN_CONV - 1, 2 * D_FF), 1.0),
        'page_table': page_table,
        'mem_prompt': nrm((BATCH, MEM_TOKENS, D_MODEL), 1.0),
        'mix_norm': gain((DEPTH, D_MODEL)),
        'l0_w_in': nrm((D_MODEL, L0_IN), D_MODEL ** -0.5),
        'l0_dn_conv_w': nrm((DN_CONV, DN_QKV), 0.5),
        'l0_dn_a_log': jnp.log(jax.random.uniform(next(keys), (DN_HEADS,), f32, 1.0, 8.0)),
        'l0_dn_dt_bias': nrm((DN_HEADS,), 0.5) - 2.0,
        'l0_dn_norm': gain((DN_DV,)),
        'l0_swa_q_norm': gain((HEAD_DIM,)),
        'l0_swa_k_norm': gain((HEAD_DIM,)),
        'l0_swa_sinks': nrm((SWA_HEADS,), 0.5),
        'l0_w_out': nrm((L0_MIX, D_MODEL), L0_MIX ** -0.5),
        'l1_w_in': nrm((D_MODEL, L1_IN), D_MODEL ** -0.5),
        'l1_q_norm': gain((HEAD_DIM,)),
        'l1_k_norm_cmp': gain((HEAD_DIM,)),
        'l1_k_norm_sel': gain((HEAD_DIM,)),
        'l1_k_norm_win': gain((HEAD_DIM,)),
        'l1_cmp_pe_k': nrm((CMP_BLK, HEAD_DIM), 0.5),
        'l1_cmp_w1_k': nrm((CMP_BLK * HEAD_DIM, HEAD_DIM), (CMP_BLK * HEAD_DIM) ** -0.5),
        'l1_cmp_w2_k': nrm((HEAD_DIM, HEAD_DIM), HEAD_DIM ** -0.5),
        'l1_cmp_pe_v': nrm((CMP_BLK, HEAD_DIM), 0.5),
        'l1_cmp_w1_v': nrm((CMP_BLK * HEAD_DIM, HEAD_DIM), (CMP_BLK * HEAD_DIM) ** -0.5),
        'l1_cmp_w2_v': nrm((HEAD_DIM, HEAD_DIM), HEAD_DIM ** -0.5),
        'l1_w_out': nrm((L1_MIX, D_MODEL), L1_MIX ** -0.5),
        'mem_norm_x': gain((DEPTH, D_MODEL)),
        'mem_norm_kv': gain((DEPTH, D_MODEL)),
        'mem_w_q': nrm((DEPTH, D_MODEL, MEM_WIDTH), D_MODEL ** -0.5),
        'mem_w_k': nrm((DEPTH, D_MODEL, MEM_WIDTH), D_MODEL ** -0.5),
        'mem_w_v': nrm((DEPTH, D_MODEL, MEM_WIDTH), D_MODEL ** -0.5),
        'mem_q_norm': gain((DEPTH, MEM_HEAD_DIM)),
        'mem_k_norm': gain((DEPTH, MEM_HEAD_DIM)),
        'mem_w_o': nrm((DEPTH, MEM_WIDTH, D_MODEL), MEM_WIDTH ** -0.5),
        'ffn_norm': gain((DEPTH, D_MODEL)),
        'ffn_w_in': nrm((DEPTH, D_MODEL, 2 * D_FF), D_MODEL ** -0.5),
        'ffn_conv_w': nrm((DEPTH, FFN_CONV, 2 * D_FF), 0.5),
        'ffn_conv_b': nrm((DEPTH, 2 * D_FF), 0.02),
        'ffn_w_out': nrm((DEPTH, D_FF, D_MODEL), D_FF ** -0.5),
    }


def reference(x_prompt, x_sample, state_dn, state_dn_conv, cache_swa_k, cache_swa_v,
              cache_cmp_k, cache_cmp_v, cache_sel_k, cache_sel_v, cache_win_k, cache_win_v,
              cache_mem_k, cache_mem_v, state_ffn_conv, page_table, mem_prompt,
              mix_norm, l0_w_in, l0_dn_conv_w, l0_dn_a_log, l0_dn_dt_bias, l0_dn_norm,
              l0_swa_q_norm, l0_swa_k_norm, l0_swa_sinks, l0_w_out,
              l1_w_in, l1_q_norm, l1_k_norm_cmp, l1_k_norm_sel, l1_k_norm_win,
              l1_cmp_pe_k, l1_cmp_w1_k, l1_cmp_w2_k, l1_cmp_pe_v, l1_cmp_w1_v, l1_cmp_w2_v, l1_w_out,
              mem_norm_x, mem_norm_kv, mem_w_q, mem_w_k, mem_w_v, mem_q_norm, mem_k_norm, mem_w_o,
              ffn_norm, ffn_w_in, ffn_conv_w, ffn_conv_b, ffn_w_out):
    P = dict(mix_norm=mix_norm, l0_w_in=l0_w_in, l0_dn_conv_w=l0_dn_conv_w, l0_dn_a_log=l0_dn_a_log,
             l0_dn_dt_bias=l0_dn_dt_bias, l0_dn_norm=l0_dn_norm, l0_swa_q_norm=l0_swa_q_norm,
             l0_swa_k_norm=l0_swa_k_norm, l0_swa_sinks=l0_swa_sinks, l0_w_out=l0_w_out,
             l1_w_in=l1_w_in, l1_q_norm=l1_q_norm, l1_k_norm_cmp=l1_k_norm_cmp, l1_k_norm_sel=l1_k_norm_sel,
             l1_k_norm_win=l1_k_norm_win, l1_cmp_pe_k=l1_cmp_pe_k, l1_cmp_w1_k=l1_cmp_w1_k,
             l1_cmp_w2_k=l1_cmp_w2_k, l1_cmp_pe_v=l1_cmp_pe_v, l1_cmp_w1_v=l1_cmp_w1_v,
             l1_cmp_w2_v=l1_cmp_w2_v, l1_w_out=l1_w_out, mem_norm_x=mem_norm_x, mem_norm_kv=mem_norm_kv,
             mem_w_q=mem_w_q, mem_w_k=mem_w_k, mem_w_v=mem_w_v, mem_q_norm=mem_q_norm,
             mem_k_norm=mem_k_norm, mem_w_o=mem_w_o, ffn_norm=ffn_norm, ffn_w_in=ffn_w_in,
             ffn_conv_w=ffn_conv_w, ffn_conv_b=ffn_conv_b, ffn_w_out=ffn_w_out)
    G, hd = NSA_KV_HEADS, HEAD_DIM

    B = x_prompt.shape[0]
    dt = x_prompt.dtype
    mem_pairs = [mem_kv(mem_prompt, layer, P) for layer in range(DEPTH)]
    p_mem_k = jnp.stack([kv[0] for kv in mem_pairs])
    p_mem_v = jnp.stack([kv[1] for kv in mem_pairs])

    def prompt_past(qn, qr, kc, vc, ks, vs):
        return lax.map(lambda a: nsa_cmp_sel(a[0], a[1], a[2], a[3], a[4], a[5], 0, P), (qn, qr, kc, vc, ks, vs))

    y_prompt, ab_p, nsa_p, p_ffn_conv = trunk(
        x_prompt, 0,
        jnp.zeros((B, DN_HEADS, DN_DK, DN_DV), dt), jnp.zeros((B, DN_CONV - 1, DN_QKV), dt),
        jnp.zeros((B, 0, SWA_KV_HEADS, HEAD_DIM), dt), jnp.zeros((B, 0, SWA_KV_HEADS, HEAD_DIM), dt),
        jnp.zeros((B, 0, G, hd), dt), jnp.zeros((B, 0, G, hd), dt),
        p_mem_k, p_mem_v, jnp.zeros((DEPTH, B, FFN_CONV - 1, 2 * D_FF), dt), prompt_past, P)
    p_dn, p_dn_conv, p_swa_k, p_swa_v = ab_p
    p_cmp_k, p_cmp_v, p_sel_k, p_sel_v, p_win_k, p_win_v = nsa_p

    def sample_past(qn, qr, kc, vc, ks, vs):
        def per_seq(a):
            pt, qn_b, qr_b, kc_b, vc_b, ks_b, vs_b = a

            def full(pool, new):
                return jnp.concatenate([pool[pt].reshape(-1, G, hd), new.astype(pool.dtype)], axis=0)

            return nsa_cmp_sel(qn_b, qr_b, full(cache_cmp_k, kc_b), full(cache_cmp_v, vc_b),
                               full(cache_sel_k, ks_b), full(cache_sel_v, vs_b), PAST_LEN, P)

        return lax.map(per_seq, (page_table, qn, qr, kc, vc, ks, vs))

    y_sample, ab_s, nsa_s, s_ffn_conv = trunk(
        x_sample, PAST_LEN, state_dn, state_dn_conv, cache_swa_k, cache_swa_v, cache_win_k, cache_win_v,
        cache_mem_k, cache_mem_v, state_ffn_conv, sample_past, P)
    s_dn, s_dn_conv, s_swa_k, s_swa_v = ab_s
    s_cmp_k, s_cmp_v, s_sel_k, s_sel_v, s_win_k, s_win_v = nsa_s

    return (y_prompt, y_sample,
            p_dn, p_dn_conv, p_swa_k, p_swa_v, p_cmp_k, p_cmp_v, p_sel_k, p_sel_v, p_win_k, p_win_v,
            p_mem_k, p_mem_v, p_ffn_conv,
            s_dn, s_dn_conv, s_swa_k, s_swa_v, s_cmp_k, s_cmp_v, s_sel_k, s_sel_v, s_win_k, s_win_v,
            s_ffn_conv)
```

```python
import functools

import numpy as np
import jax
import jax.numpy as jnp
from jax import lax
from jax.experimental import pallas as pl
from jax.experimental.pallas import tpu as pltpu

F32 = jnp.float32
BF16 = jnp.bfloat16
HI = lax.Precision.HIGHEST
NEG_INF = float("-inf")

EPS = 1e-6
HEAD_DIM = 64
ROT_DIM = HEAD_DIM // 4
ROPE_THETA = 500000.0
QBLK = 128
DN_HEADS = 4
DN_DK = 128
DN_CHUNK = 64
SWA_HEADS = 8
SWA_KV_HEADS = 2
SWA_WINDOW = 128
NSA_HEADS = 16
NSA_KV_HEADS = 4
CMP_BLK = 32
CMP_STRIDE = 16
SEL_BLK = 64
N_SEL = 16
NSA_WINDOW = 512
SEL_BONUS = 1e4
MEM_HEADS = 4
MEM_HEAD_DIM = 128

LANES = 128
SUBLANES = 8
VMEM_LIMIT = 56 << 20
PAGES_PER_STEP = 32


def _cparams(*sem):
    return pltpu.CompilerParams(dimension_semantics=sem, vmem_limit_bytes=VMEM_LIMIT)


def _nt(a, b, precision=None):
    return lax.dot_general(a, b, (((1,), (1,)), ((), ())), precision=precision,
                           preferred_element_type=F32)


def _tn(a, b, precision=None):
    return lax.dot_general(a, b, (((0,), (0,)), ((), ())), precision=precision,
                           preferred_element_type=F32)


def _dot(a, b, precision=None):
    return jnp.dot(a, b, precision=precision, preferred_element_type=F32)


def _silu(x):
    return x / (1.0 + jnp.exp(-x))


def _sigmoid(x):
    return 1.0 / (1.0 + jnp.exp(-x))


def _softplus(x):
    return jnp.maximum(x, 0.0) + jnp.log(1.0 + jnp.exp(-jnp.abs(x)))


def _split3(x):
    a = x.astype(BF16)
    r = x - a.astype(F32)
    b = r.astype(BF16)
    c = (r - b.astype(F32)).astype(BF16)
    return a, b, c


def _row_tile(m, pref):
    for t in (1024, 512, 256, 128, 64, 32, 16, 8):
        if t <= pref and m % t == 0:
            return t
    return m


def _shift_rows(cur, halo, k):
    r = pltpu.roll(cur, k, 0)
    h = pltpu.roll(halo, k, 0)
    row = lax.broadcasted_iota(jnp.int32, h.shape, 0)
    top = jnp.where(row < k, h, r[0:SUBLANES])
    if cur.shape[0] == SUBLANES:
        return top
    return jnp.concatenate([top, r[SUBLANES:]], axis=0)


def _proj_kernel(*refs, n_add, has_gain, has_res, seg):
    it = iter(refs)
    adds = [next(it) for _ in range(n_add)]
    gain = next(it) if has_gain else None
    w = next(it)
    res = next(it) if has_res else None
    outs = [next(it) for _ in seg]
    xn = next(it)

    @pl.when(pl.program_id(1) == 0)
    def _():
        x = adds[0][...]
        for a in adds[1:]:
            x = x + a[...]
        if has_gain:
            ms = jnp.mean(x * x, axis=-1, keepdims=True)
            x = x * lax.rsqrt(ms + EPS) * gain[...]
        xn[...] = x.astype(BF16)

    acc = _dot(xn[...], w[...])
    if has_res:
        acc = acc + res[...]
    if len(seg) == 1:
        outs[0][...] = acc
    else:
        off = 0
        for o, s in zip(outs, seg):
            o[...] = acc[:, off:off + s]
            off += s


def _proj(adds, w, *, gain=None, res=None, seg=None, tn=None, tm_pref=512):
    m, k = adds[0].shape
    n = w.shape[1]
    tm = _row_tile(m, tm_pref)
    tn = n if tn is None else tn
    seg = (n,) if seg is None else tuple(seg)
    assert n % tn == 0 and (len(seg) == 1 or tn == n) and sum(seg) == n
    in_specs = [pl.BlockSpec((tm, k), lambda i, j: (i, 0)) for _ in adds]
    args = list(adds)
    if gain is not None:
        in_specs.append(pl.BlockSpec((1, k), lambda i, j: (0, 0)))
        args.append(gain.reshape(1, k))
    in_specs.append(pl.BlockSpec((k, tn), lambda i, j: (0, j)))
    args.append(w)
    if res is not None:
        in_specs.append(pl.BlockSpec((tm, tn), lambda i, j: (i, j)))
        args.append(res)
    if len(seg) == 1:
        out_shape = [jax.ShapeDtypeStruct((m, n), F32)]
        out_specs = [pl.BlockSpec((tm, tn), lambda i, j: (i, j))]
    else:
        out_shape = [jax.ShapeDtypeStruct((m, s), F32) for s in seg]
        out_specs = [pl.BlockSpec((tm, s), lambda i, j: (i, 0)) for s in seg]
    outs = pl.pallas_call(
        functools.partial(_proj_kernel, n_add=len(adds), has_gain=gain is not None,
                          has_res=res is not None, seg=seg),
        grid=(m // tm, n // tn), in_specs=in_specs, out_specs=out_specs, out_shape=out_shape,
        scratch_shapes=[pltpu.VMEM((tm, k), BF16)],
        compiler_params=_cparams("parallel", "arbitrary"), name="proj")(*args)
    return outs[0] if len(seg) == 1 else outs


def _hn_kernel(*refs, hd, want_norm, want_rope):
    x_ref, g_ref = refs[0], refs[1]
    i = 2
    if want_rope:
        c_ref, sa_ref, sb_ref = refs[i:i + 3]
        i += 3
    outs = refs[i:]
    width = x_ref.shape[1]
    r = lax.broadcasted_iota(jnp.int32, (LANES, LANES), 0) // hd
    c = lax.broadcasted_iota(jnp.int32, (LANES, LANES), 1) // hd
    bd = jnp.where(r == c, 1.0, 0.0).astype(BF16)
    for cb in range(width // LANES):
        sl = slice(cb * LANES, (cb + 1) * LANES)
        xc = x_ref[:, sl]
        x2 = xc * xc
        hi = x2.astype(BF16)
        lo = (x2 - hi.astype(F32)).astype(BF16)
        ss = _dot(hi, bd) + _dot(lo, bd)
        xc = xc * lax.rsqrt(ss * (1.0 / hd) + EPS) * g_ref[:, sl]
        k = 0
        if want_norm:
            outs[k][:, sl] = xc
            k += 1
        if want_rope:
            outs[k][:, sl] = (xc * c_ref[...] + pltpu.roll(xc, LANES - ROT_DIM // 2, 1) * sa_ref[...]
                              + pltpu.roll(xc, ROT_DIM // 2, 1) * sb_ref[...])


def _rope_tables(pos):
    half = ROT_DIM // 2
    inv = ROPE_THETA ** (-jnp.arange(half, dtype=F32) / half)
    ang = pos.astype(F32)[:, None] * inv[None, :]
    jj = np.arange(LANES) % HEAD_DIM
    cos = jnp.cos(ang)[:, jj % half]
    sin = jnp.sin(ang)[:, jj % half]
    c = jnp.where(jj[None, :] < ROT_DIM, cos, 1.0)
    sa = jnp.where(jj[None, :] < half, -sin, 0.0)
    sb = jnp.where((jj[None, :] >= half) & (jj[None, :] < ROT_DIM), sin, 0.0)
    return c, sa, sb


def _headnorm(x, gain, hd, *, tabs=None, seq_len=None, want_norm=True):
    m, width = x.shape
    want_rope = tabs is not None
    tm = _row_tile(seq_len if (want_rope and tabs[0].shape[0] > 1) else m, 512)
    g = jnp.tile(gain.reshape(1, hd), (1, width // hd))
    in_specs = [pl.BlockSpec((tm, width), lambda i: (i, 0)), pl.BlockSpec((1, width), lambda i: (0, 0))]
    args = [x, g]
    if want_rope:
        if tabs[0].shape[0] > 1:
            nt = seq_len // tm
            tspec = pl.BlockSpec((tm, LANES), lambda i: (i % nt, 0))
        else:
            tspec = pl.BlockSpec((1, LANES), lambda i: (0, 0))
        in_specs += [tspec] * 3
        args += list(tabs)
    n_out = int(want_norm) + int(want_rope)
    outs = pl.pallas_call(
        functools.partial(_hn_kernel, hd=hd, want_norm=want_norm, want_rope=want_rope),
        grid=(m // tm,), in_specs=in_specs,
        out_specs=[pl.BlockSpec((tm, width), lambda i: (i, 0))] * n_out,
        out_shape=[jax.ShapeDtypeStruct((m, width), F32)] * n_out,
        compiler_params=_cparams("parallel"), name="headnorm")(*args)
    return outs[0] if n_out == 1 else outs


def _dn_gates(ba, pa_ref, pd_ref):
    beta = _sigmoid(ba)
    g = -jnp.exp(pa_ref[...]) * _softplus(ba + pd_ref[...])
    return beta, g


def _dna_kernel(qkv_ref, halo_ref, ba_ref, cw_ref, pa_ref, pd_ref,
                w_o, u_o, qg_o, kd_o, qk_o, gc_o, *, C):
    n = pl.program_id(1)
    x = qkv_ref[0]
    halo = jnp.where(n == 0, 0.0, halo_ref[0])
    c = (x * cw_ref[3:4, :] + _shift_rows(x, halo, 1) * cw_ref[2:3, :]
         + _shift_rows(x, halo, 2) * cw_ref[1:2, :] + _shift_rows(x, halo, 3) * cw_ref[0:1, :])
    c = _silu(c)
    beta_all, gmat = _dn_gates(ba_ref[0], pa_ref, pd_ref)
    ri = lax.broadcasted_iota(jnp.int32, (C, C), 0)
    ci = lax.broadcasted_iota(jnp.int32, (C, C), 1)
    lower = ri >= ci
    tri = jnp.where(lower, 1.0, 0.0)
    eye = jnp.where(ri == ci, 1.0, 0.0)
    gc = _dot(tri, gmat, HI)
    gc_o[0] = gc
    er = lax.broadcasted_iota(jnp.int32, (SUBLANES, LANES), 0)
    ec = lax.broadcasted_iota(jnp.int32, (SUBLANES, LANES), 1)
    esel = jnp.where(ec == er + DN_HEADS, 1.0, 0.0)
    gcr_all = _nt(esel, gc, HI)
    n_fac = max(int(np.ceil(np.log2(C))) - 1, 0)
    dk = DN_DK
    for h in range(DN_HEADS):
        qh = c[:, h * dk:(h + 1) * dk]
        kh = c[:, (DN_HEADS + h) * dk:(DN_HEADS + h + 1) * dk]
        vh = c[:, (2 * DN_HEADS + h) * dk:(2 * DN_HEADS + h + 1) * dk]
        qh = qh * lax.rsqrt(jnp.sum(qh * qh, axis=-1, keepdims=True) + EPS) * (dk ** -0.5)
        kh = kh * lax.rsqrt(jnp.sum(kh * kh, axis=-1, keepdims=True) + EPS)
        bh = beta_all[:, h:h + 1]
        gcc = gc[:, DN_HEADS + h:DN_HEADS + h + 1]
        gcr = gcr_all[h:h + 1, :]
        decay = jnp.where(lower, jnp.exp(jnp.where(lower, gcc - gcr, 0.0)), 0.0)
        kb = kh * bh
        lmat = jnp.where(ri > ci, _nt(kb, kh, HI) * decay, 0.0)
        tinv = eye - lmat
        if n_fac > 0:
            p = _dot(lmat, lmat, HI)
            for f in range(n_fac):
                tinv = tinv + _dot(tinv, p, HI)
                if f + 1 < n_fac:
                    p = _dot(p, p, HI)
        eg = jnp.exp(gcc)
        wu = _dot(tinv, jnp.concatenate([kb * eg, vh * bh], axis=1), HI)
        sl = slice(h * dk, (h + 1) * dk)
        w_o[0, :, sl] = wu[:, :dk]
        u_o[0, :, sl] = wu[:, dk:]
        qg_o[0, :, sl] = qh * eg
        kd_o[0, :, sl] = kh * jnp.exp(gcc[C - 1:C, :] - gcc)
        qk_o[0, :, h * C:(h + 1) * C] = jnp.where(lower, _nt(qh, kh, HI) * decay, 0.0)


def _dn_prepare(qkv3, ba3, cw8, pa, pd):
    b, t, wq = qkv3.shape
    C = DN_CHUNK
    assert t % C == 0
    nh = DN_HEADS * DN_DK
    hb = C // SUBLANES
    outs = pl.pallas_call(
        functools.partial(_dna_kernel, C=C),
        grid=(b, t // C),
        in_specs=[pl.BlockSpec((1, C, wq), lambda i, n: (i, n, 0)),
                  pl.BlockSpec((1, SUBLANES, wq), lambda i, n: (i, jnp.maximum(n * hb - 1, 0), 0)),
                  pl.BlockSpec((1, C, LANES), lambda i, n: (i, n, 0)),
                  pl.BlockSpec((SUBLANES, wq), lambda i, n: (0, 0)),
                  pl.BlockSpec((1, LANES), lambda i, n: (0, 0)),
                  pl.BlockSpec((1, LANES), lambda i, n: (0, 0))],
        out_specs=[pl.BlockSpec((1, C, nh), lambda i, n: (i, n, 0))] * 4
        + [pl.BlockSpec((1, C, DN_HEADS * C), lambda i, n: (i, n, 0)),
           pl.BlockSpec((1, C, LANES), lambda i, n: (i, n, 0))],
        out_shape=[jax.ShapeDtypeStruct((b, t, nh), F32)] * 4
        + [jax.ShapeDtypeStruct((b, t, DN_HEADS * C), F32), jax.ShapeDtypeStruct((b, t, LANES), F32)],
        compiler_params=_cparams("parallel", "parallel"), name="dn_prepare")(qkv3, qkv3, ba3, cw8, pa, pd)
    return outs


def _dnb_kernel(w_ref, u_ref, qg_ref, kd_ref, qk_ref, gc_ref, z_ref, gn_ref, o_ref, s_out, s_ref, *, C, nb):
    n = pl.program_id(0)

    @pl.when(n == 0)
    def _():
        s_ref[...] = jnp.zeros_like(s_ref)

    dk = DN_DK
    for b in range(nb):
        for h in range(DN_HEADS):
            sl = slice(h * dk, (h + 1) * dk)
            s = s_ref[b * DN_HEADS + h]
            v_new = u_ref[b, :, sl] - _dot(w_ref[b, :, sl], s, HI)
            o = _dot(qg_ref[b, :, sl], s, HI) + _dot(qk_ref[b, :, h * C:(h + 1) * C], v_new, HI)
            dl = jnp.exp(gc_ref[b, C - 1:C, DN_HEADS + h:DN_HEADS + h + 1])
            s_ref[b * DN_HEADS + h] = s * dl + _tn(kd_ref[b, :, sl], v_new, HI)
            o = o * lax.rsqrt(jnp.mean(o * o, axis=-1, keepdims=True) + EPS) * gn_ref[...]
            o_ref[b, :, sl] = o * _silu(z_ref[b, :, sl])

    @pl.when(n == pl.num_programs(0) - 1)
    def _():
        for b in range(nb):
            for h in range(DN_HEADS):
                s_out[b, h] = s_ref[b * DN_HEADS + h]


def _dn_recur(w, u, qg, kd, qk, gc, z3, gn):
    b, t, nh = w.shape
    C = DN_CHUNK
    big = pl.BlockSpec((b, C, nh), lambda n: (0, n, 0))
    return pl.pallas_call(
        functools.partial(_dnb_kernel, C=C, nb=b),
        grid=(t // C,),
        in_specs=[big, big, big, big,
                  pl.BlockSpec((b, C, DN_HEADS * C), lambda n: (0, n, 0)),
                  pl.BlockSpec((b, C, LANES), lambda n: (0, n, 0)),
                  big, pl.BlockSpec((1, DN_DK), lambda n: (0, 0))],
        out_specs=[big, pl.BlockSpec((b, DN_HEADS, DN_DK, DN_DK), lambda n: (0, 0, 0, 0))],
        out_shape=[jax.ShapeDtypeStruct((b, t, nh), F32),
                   jax.ShapeDtypeStruct((b, DN_HEADS, DN_DK, DN_DK), F32)],
        scratch_shapes=[pltpu.VMEM((b * DN_HEADS, DN_DK, DN_DK), F32)],
        compiler_params=_cparams("arbitrary"), name="dn_recur")(w, u, qg, kd, qk, gc, z3, gn)


def _dns_kernel(qkv_ref, st_ref, ba_ref, z_ref, s_ref, cw_ref, pa_ref, pd_ref, gn_ref,
                o_ref, so_ref, sto_ref):
    new = qkv_ref[0]
    st = st_ref[0]
    c = (st[0:1] * cw_ref[0:1, :] + st[1:2] * cw_ref[1:2, :] + st[2:3] * cw_ref[2:3, :]
         + new * cw_ref[3:4, :])
    c = _silu(c)
    sto_ref[0, 0:2, :] = st[1:3]
    sto_ref[0, 2:3, :] = new
    beta_all, gmat = _dn_gates(ba_ref[0], pa_ref, pd_ref)
    dk = DN_DK
    row = lax.broadcasted_iota(jnp.int32, (SUBLANES, dk), 0)
    for h in range(DN_HEADS):
        qh = c[:, h * dk:(h + 1) * dk]
        kh = c[:, (DN_HEADS + h) * dk:(DN_HEADS + h + 1) * dk]
        vh = c[:, (2 * DN_HEADS + h) * dk:(2 * DN_HEADS + h + 1) * dk]
        qh = qh * lax.rsqrt(jnp.sum(qh * qh, axis=-1, keepdims=True) + EPS) * (dk ** -0.5)
        kh = kh * lax.rsqrt(jnp.sum(kh * kh, axis=-1, keepdims=True) + EPS)
        bh = beta_all[:, h:h + 1]
        eg = jnp.exp(gmat[:, DN_HEADS + h:DN_HEADS + h + 1])
        s = s_ref[0, h]
        kq = jnp.where(row == 0, kh, jnp.where(row == 1, qh, 0.0))
        ks_qs = _dot(kq, s, HI)
        v_new = bh * (vh - eg * ks_qs[0:1])
        o = eg * ks_qs[1:2] + jnp.sum(qh * kh, axis=-1, keepdims=True) * v_new
        k8 = jnp.where(row == 0, kh, 0.0)
        v8 = jnp.where(row == 0, v_new, 0.0)
        so_ref[0, h] = s * eg + _tn(k8, v8, HI)
        o = o * lax.rsqrt(jnp.mean(o * o, axis=-1, keepdims=True) + EPS) * gn_ref[...]
        o_ref[0, :, h * dk:(h + 1) * dk] = o * _silu(z_ref[0, :, h * dk:(h + 1) * dk])


def _dn_step(qkv, conv_state, ba, z, state, cw8, pa, pd, gn):
    b, wq = qkv.shape
    nh = DN_HEADS * DN_DK
    one = lambda i: (i, 0, 0)
    cst = lambda i: (0, 0)
    return pl.pallas_call(
        _dns_kernel, grid=(b,),
        in_specs=[pl.BlockSpec((1, 1, wq), one), pl.BlockSpec((1, 3, wq), one),
                  pl.BlockSpec((1, 1, LANES), one), pl.BlockSpec((1, 1, nh), one),
                  pl.BlockSpec((1, DN_HEADS, DN_DK, DN_DK), lambda i: (i, 0, 0, 0)),
                  pl.BlockSpec((SUBLANES, wq), cst), pl.BlockSpec((1, LANES), cst),
                  pl.BlockSpec((1, LANES), cst), pl.BlockSpec((1, DN_DK), cst)],
        out_specs=[pl.BlockSpec((1, 1, nh), one),
                   pl.BlockSpec((1, DN_HEADS, DN_DK, DN_DK), lambda i: (i, 0, 0, 0)),
                   pl.BlockSpec((1, 3, wq), one)],
        out_shape=[jax.ShapeDtypeStruct((b, 1, nh), F32),
                   jax.ShapeDtypeStruct(state.shape, F32),
                   jax.ShapeDtypeStruct(conv_state.shape, F32)],
        compiler_params=_cparams("parallel"), name="dn_step")(
            qkv.reshape(b, 1, wq), conv_state, ba.reshape(b, 1, LANES), z.reshape(b, 1, nh),
            state, cw8, pa, pd, gn)


def _band_kernel(*refs, H, G, window, tq, span, T, has_sink, gate_col):
    q_ref, k_ref, v_ref = refs[:3]
    i = 3
    sink_ref = gate_ref = None
    if has_sink:
        sink_ref = refs[i]
        i += 1
    if gate_col is not None:
        gate_ref = refs[i]
        i += 1
    o_ref = refs[i]
    qi = pl.program_id(1)
    start = jnp.minimum(jnp.maximum(qi * tq - window, 0), T - span)
    start = pl.multiple_of(start, tq)
    kb = k_ref[0, pl.ds(start, span), :]
    vb = v_ref[0, pl.ds(start, span), :]
    qpos = qi * tq + lax.broadcasted_iota(jnp.int32, (tq, span), 0)
    kpos = start + lax.broadcasted_iota(jnp.int32, (tq, span), 1)
    d = qpos - kpos
    mask = (d >= 0) & (d < window)
    R = H // G
    hd = HEAD_DIM
    scale = hd ** -0.5
    for g in range(G):
        kg = kb[:, g * hd:(g + 1) * hd].astype(BF16)
        vg = vb[:, g * hd:(g + 1) * hd].astype(BF16)
        for r in range(R):
            h = g * R + r
            qh = (q_ref[0, :, h * hd:(h + 1) * hd] * scale).astype(BF16)
            s = jnp.where(mask, _nt(qh, kg), NEG_INF)
            m = jnp.max(s, axis=-1, keepdims=True)
            if has_sink:
                sk = sink_ref[0:1, h:h + 1]
                m = jnp.maximum(m, sk)
                e = jnp.exp(s - m)
                den = jnp.sum(e, axis=-1, keepdims=True) + jnp.exp(sk - m)
            else:
                m = jnp.where(m == NEG_INF, 0.0, m)
                e = jnp.exp(s - m)
                den = jnp.maximum(jnp.sum(e, axis=-1, keepdims=True), 1e-30)
            o = _dot((e / den).astype(BF16), vg)
            if gate_ref is not None:
                o = o * _sigmoid(gate_ref[0, :, 3 * h + gate_col:3 * h + gate_col + 1])
            o_ref[0, :, h * hd:(h + 1) * hd] = o


def _band_attn(q3, k3, v3, *, H, G, window, sinks=None, gate3=None, gate_col=None):
    b, t, _ = q3.shape
    tq = min(QBLK, t)
    assert t % tq == 0
    span = min(window + tq, t)
    in_specs = [pl.BlockSpec((1, tq, H * HEAD_DIM), lambda i, j: (i, j, 0)),
                pl.BlockSpec((1, t, G * HEAD_DIM), lambda i, j: (i, 0, 0)),
                pl.BlockSpec((1, t, G * HEAD_DIM), lambda i, j: (i, 0, 0))]
    args = [q3, k3, v3]
    if sinks is not None:
        in_specs.append(pl.BlockSpec((1, LANES), lambda i, j: (0, 0)))
        args.append(jnp.pad(sinks.astype(F32), (0, LANES - H)).reshape(1, LANES))
    if gate3 is not None:
        in_specs.append(pl.BlockSpec((1, tq, LANES), lambda i, j: (i, j, 0)))
        args.append(gate3)
    else:
        gate_col = None
    return pl.pallas_call(
        functools.partial(_band_kernel, H=H, G=G, window=window, tq=tq, span=span, T=t,
                          has_sink=sinks is not None, gate_col=gate_col),
        grid=(b, t // tq), in_specs=in_specs,
        out_specs=pl.BlockSpec((1, tq, H * HEAD_DIM), lambda i, j: (i, j, 0)),
        out_shape=jax.ShapeDtypeStruct((b, t, H * HEAD_DIM), F32),
        compiler_params=_cparams("parallel", "parallel"), name="band_attn")(*args)


def _swin_kernel(*refs, H, G, W, has_sink, gate_col):
    q_ref, ck_ref, cv_ref, nk_ref, nv_ref = refs[:5]
    i = 5
    sink_ref = gate_ref = None
    if has_sink:
        sink_ref = refs[i]
        i += 1
    if gate_col is not None:
        gate_ref = refs[i]
        i += 1
    ok_ref, ov_ref, o_ref = refs[i:i + 3]
    row = lax.broadcasted_iota(jnp.int32, (W, G * HEAD_DIM), 0)
    kn = jnp.where(row == W - 1, nk_ref[0], pltpu.roll(ck_ref[0], W - 1, 0))
    vn = jnp.where(row == W - 1, nv_ref[0], pltpu.roll(cv_ref[0], W - 1, 0))
    ok_ref[0] = kn
    ov_ref[0] = vn
    R = H // G
    hd = HEAD_DIM
    scale = hd ** -0.5
    for g in range(G):
        kg = kn[:, g * hd:(g + 1) * hd].astype(BF16)
        vg = vn[:, g * hd:(g + 1) * hd].astype(BF16)
        qg = (q_ref[0, g * R:(g + 1) * R, :] * scale).astype(BF16)
        s = _nt(qg, kg)
        m = jnp.max(s, axis=-1, keepdims=True)
        if has_sink:
            sk = sink_ref[g * R:(g + 1) * R, :]
            m = jnp.maximum(m, sk)
            e = jnp.exp(s - m)
            den = jnp.sum(e, axis=-1, keepdims=True) + jnp.exp(sk - m)
        else:
            e = jnp.exp(s - m)
            den = jnp.maximum(jnp.sum(e, axis=-1, keepdims=True), 1e-30)
        o = _dot((e / den).astype(BF16), vg)
        if gate_ref is not None:
            o = o * _sigmoid(gate_ref[0, g * R:(g + 1) * R, gate_col:gate_col + 1])
        o_ref[0, g * R:(g + 1) * R, :] = o


def _step_window_attn(q, cache_k, cache_v, new_k, new_v, *, H, G, sinks=None, gate=None, gate_col=None):
    b, W, gd = cache_k.shape
    one = lambda i: (i, 0, 0)
    in_specs = [pl.BlockSpec((1, H, HEAD_DIM), one), pl.BlockSpec((1, W, gd), one),
                pl.BlockSpec((1, W, gd), one), pl.BlockSpec((1, 1, gd), one), pl.BlockSpec((1, 1, gd), one)]
    args = [q, cache_k, cache_v, new_k.reshape(b, 1, gd), new_v.reshape(b, 1, gd)]
    if sinks is not None:
        in_specs.append(pl.BlockSpec((H, 1), lambda i: (0, 0)))
        args.append(sinks.astype(F32).reshape(H, 1))
    if gate is not None:
        in_specs.append(pl.BlockSpec((1, H, 3), one))
        args.append(gate)
    else:
        gate_col = None
    return pl.pallas_call(
        functools.partial(_swin_kernel, H=H, G=G, W=W, has_sink=sinks is not None, gate_col=gate_col),
        grid=(b,), in_specs=in_specs,
        out_specs=[pl.BlockSpec((1, W, gd), one), pl.BlockSpec((1, W, gd), one),
                   pl.BlockSpec((1, H, HEAD_DIM), one)],
        out_shape=[jax.ShapeDtypeStruct((b, W, gd), F32), jax.ShapeDtypeStruct((b, W, gd), F32),
                   jax.ShapeDtypeStruct((b, H, HEAD_DIM), F32)],
        compiler_params=_cparams("parallel"), name="step_window_attn")(*args)


def _cmp_core(xs_ref, wp_ref, w2_ref, gain_ref, out_ref, *, r0):
    rt = r0 + 2 * SUBLANES
    rd = r0 + SUBLANES
    lane = lax.broadcasted_iota(jnp.int32, (1, LANES), 1)
    out = jnp.zeros((r0, NSA_KV_HEADS * HEAD_DIM), F32)
    for gp in range(2):
        acc = jnp.zeros((rt, 2 * LANES), F32)
        for ip in range(CMP_STRIDE // 2):
            a = xs_ref[:, (4 * ip + gp) * LANES:(4 * ip + gp + 1) * LANES]
            b = xs_ref[:, (4 * ip + 2 + gp) * LANES:(4 * ip + 3 + gp) * LANES]
            acc = acc + _dot(jnp.concatenate([a, b], axis=1).astype(BF16), wp_ref[ip])
        for gl in range(2):
            piece = acc[:, gl * LANES:(gl + 1) * LANES]
            bias = jnp.where(lane < HEAD_DIM, piece[rd:rd + 1], piece[rd + 1:rd + 2])
            data = piece[:rd] + bias
            pre = data + pltpu.roll(pltpu.roll(data, rd - 1, 0), HEAD_DIM, 1)
            hcat = _silu(pre)[:r0]
            out = out + _dot(hcat.astype(BF16), w2_ref[2 * gp + gl])
    if gain_ref is not None:
        n = NSA_KV_HEADS * HEAD_DIM
        r = lax.broadcasted_iota(jnp.int32, (n, n), 0) // HEAD_DIM
        c = lax.broadcasted_iota(jnp.int32, (n, n), 1) // HEAD_DIM
        bd = jnp.where(r == c, 1.0, 0.0).astype(BF16)
        x2 = out * out
        hi = x2.astype(BF16)
        lo = (x2 - hi.astype(F32)).astype(BF16)
        ss = _dot(hi, bd) + _dot(lo, bd)
        out = out * lax.rsqrt(ss * (1.0 / HEAD_DIM) + EPS) * gain_ref[...]
    out_ref[0] = out


def _cmp_seq_kernel(*refs, r0, has_norm):
    x_ref, pe_ref, wp_ref, w2_ref = refs[:4]
    gain_ref = refs[4] if has_norm else None
    out_ref, xs_ref = refs[-2], refs[-1]
    xs_ref[0:r0, :] = x_ref[0]
    xs_ref[r0:r0 + SUBLANES, :] = jnp.zeros((SUBLANES, xs_ref.shape[1]), F32)
    xs_ref[r0 + SUBLANES:r0 + 2 * SUBLANES, :] = pe_ref[...]
    _cmp_core(xs_ref, wp_ref, w2_ref, gain_ref, out_ref, r0=r0)


def _cmp_paged_kernel(*refs, r0, n_pages, has_norm):
    pages = refs[1:1 + n_pages + 1]
    i = n_pages + 2
    pe_ref, wp_ref, w2_ref = refs[i:i + 3]
    gain_ref = refs[i + 3] if has_norm else None
    out_ref, xs_ref = refs[-2], refs[-1]
    for j, p in enumerate(pages):
        xs_ref[j * SUBLANES:(j + 1) * SUBLANES, :] = p[0]
    xs_ref[r0 + SUBLANES:r0 + 2 * SUBLANES, :] = pe_ref[...]
    _cmp_core(xs_ref, wp_ref, w2_ref, gain_ref, out_ref, r0=r0)


def _cmp_weights(pe, w1, w2):
    hd, G = HEAD_DIM, NSA_KV_HEADS
    w1r = w1.reshape(2, CMP_STRIDE // 2, 2, hd, hd)
    wp = jnp.einsum('ab,hpidn->piadbhn', jnp.eye(2, dtype=F32), w1r)
    wp = wp.reshape(CMP_STRIDE // 2, 4 * hd, 4 * hd).astype(BF16)
    w2g = jnp.zeros((G, LANES, G * hd), F32)
    for g in range(G):
        w2g = w2g.at[g, :hd, g * hd:(g + 1) * hd].set(w2)
    pe_rows = jnp.zeros((SUBLANES, CMP_STRIDE * G * hd), F32)
    for half in range(2):
        tile = jnp.broadcast_to(pe[half * CMP_STRIDE:(half + 1) * CMP_STRIDE, None, :], (CMP_STRIDE, G, hd))
        pe_rows = pe_rows.at[half].set(tile.reshape(-1))
    return pe_rows, wp, w2g.astype(BF16)


def _cmp_specs(has_norm, cst):
    gd = NSA_KV_HEADS * HEAD_DIM
    specs = [pl.BlockSpec((SUBLANES, CMP_STRIDE * gd), cst),
             pl.BlockSpec((CMP_STRIDE // 2, 4 * HEAD_DIM, 4 * HEAD_DIM), lambda *a: (0, 0, 0)),
             pl.BlockSpec((NSA_KV_HEADS, LANES, gd), lambda *a: (0, 0, 0))]
    if has_norm:
        specs.append(pl.BlockSpec((1, gd), cst))
    return specs


def _compress_seq(x3, pe, w1, w2, gain=None):
    b, t, gd = x3.shape
    r0 = t // CMP_STRIDE
    xc = x3.reshape(b, r0, CMP_STRIDE * gd)
    pe_rows, wp, w2g = _cmp_weights(pe, w1, w2)
    cst = lambda i: (0, 0)
    args = [xc, pe_rows, wp, w2g]
    if gain is not None:
        args.append(jnp.tile(gain.reshape(1, HEAD_DIM), (1, NSA_KV_HEADS)))
    return pl.pallas_call(
        functools.partial(_cmp_seq_kernel, r0=r0, has_norm=gain is not None),
        grid=(b,),
        in_specs=[pl.BlockSpec((1, r0, CMP_STRIDE * gd), lambda i: (i, 0, 0))] + _cmp_specs(gain is not None, cst),
        out_specs=pl.BlockSpec((1, r0, gd), lambda i: (i, 0, 0)),
        out_shape=jax.ShapeDtypeStruct((b, r0, gd), F32),
        scratch_shapes=[pltpu.VMEM((r0 + 2 * SUBLANES, CMP_STRIDE * gd), F32)],
        compiler_params=_cparams("parallel"), name="compress_seq")(*args)


def _compress_paged(cache, page_table, pe, w1, w2, gain=None):
    n_pool, page, G, hd = cache.shape
    gd = G * hd
    b, n_pages = page_table.shape
    rows_pp = page // CMP_STRIDE
    assert rows_pp == SUBLANES and n_pages % PAGES_PER_STEP == 0
    r0 = PAGES_PER_STEP * rows_pp
    nq = n_pages // PAGES_PER_STEP
    cv = cache.reshape(n_pool, rows_pp, CMP_STRIDE * gd)
    pe_rows, wp, w2g = _cmp_weights(pe, w1, w2)
    cst = lambda i, q, pt: (0, 0)

    def page_spec(j):
        return pl.BlockSpec((1, rows_pp, CMP_STRIDE * gd),
                            lambda i, q, pt: (pt[i * n_pages + jnp.minimum(q * PAGES_PER_STEP + j, n_pages - 1)], 0, 0))

    in_specs = [page_spec(j) for j in range(PAGES_PER_STEP + 1)] + _cmp_specs(gain is not None, cst)
    args = [cv] * (PAGES_PER_STEP + 1) + [pe_rows, wp, w2g]
    if gain is not None:
        args.append(jnp.tile(gain.reshape(1, HEAD_DIM), (1, NSA_KV_HEADS)))
    gs = pltpu.PrefetchScalarGridSpec(
        num_scalar_prefetch=1, grid=(b, nq), in_specs=in_specs,
        out_specs=pl.BlockSpec((1, r0, gd), lambda i, q, pt: (i, q, 0)),
        scratch_shapes=[pltpu.VMEM((r0 + 2 * SUBLANES, CMP_STRIDE * gd), F32)])
    return pl.pallas_call(
        functools.partial(_cmp_paged_kernel, r0=r0, n_pages=PAGES_PER_STEP, has_norm=gain is not None),
        grid_spec=gs, out_shape=jax.ShapeDtypeStruct((b, nq * r0, gd), F32),
        compiler_params=_cparams("parallel", "arbitrary"), name="compress_paged")(
            page_table.reshape(-1), *args)


def _rank_rows(imp_ref, n_cand):
    shape = imp_ref.shape
    x = imp_ref[...]
    ridx = lax.broadcasted_iota(jnp.int32, shape, 0)

    def body(sp, cnt):
        row = imp_ref[pl.ds(sp, 1), :]
        ge = jnp.where(row >= x, 1.0, 0.0)
        gt = jnp.where(row > x, 1.0, 0.0)
        return cnt + jnp.where(sp < ridx, ge, gt)

    return lax.fori_loop(0, n_cand, body, jnp.zeros(shape, F32))


def _rank_kernel(imp_ref, rank_ref, *, n_cand):
    rank_ref[...] = _rank_rows(imp_ref, n_cand)


def _rank_call(imp_t, n_cand):
    return pl.pallas_call(
        functools.partial(_rank_kernel, n_cand=n_cand),
        out_shape=jax.ShapeDtypeStruct(imp_t.shape, F32),
        compiler_params=pltpu.CompilerParams(vmem_limit_bytes=VMEM_LIMIT), name="rank")(imp_t)


def _cmpattn_kernel(q_ref, kc_ref, vc_ref, gate_ref, cov_ref, o_ref, sel_ref, *scratch,
                    tq, n_cmp, n_sel, k_top, pos0, seq_mode):
    qi = pl.program_id(1)
    nc = kc_ref.shape[1]
    hd = HEAD_DIM
    G, R = NSA_KV_HEADS, NSA_HEADS // NSA_KV_HEADS
    scale = hd ** -0.5
    t_col = pos0 + qi * tq + lax.broadcasted_iota(jnp.int32, (tq, nc), 0)
    cidx = lax.broadcasted_iota(jnp.int32, (tq, nc), 1)
    mask = (cidx * CMP_STRIDE + CMP_BLK - 1 <= t_col) & (cidx < n_cmp)
    for g in range(G):
        kg = kc_ref[0, :, g * hd:(g + 1) * hd].astype(BF16)
        vg = vc_ref[0, :, g * hd:(g + 1) * hd].astype(BF16)
        psum = jnp.zeros((tq, nc), F32)
        for r in range(R):
            h = g * R + r
            qh = (q_ref[0, :, h * hd:(h + 1) * hd] * scale).astype(BF16)
            s = jnp.where(mask, _nt(qh, kg), NEG_INF)
            m = jnp.max(s, axis=-1, keepdims=True)
            m = jnp.where(m == NEG_INF, 0.0, m)
            e = jnp.exp(s - m)
            p = e / jnp.maximum(jnp.sum(e, axis=-1, keepdims=True), 1e-30)
            o = _dot(p.astype(BF16), vg)
            o_ref[0, :, h * hd:(h + 1) * hd] = o * _sigmoid(gate_ref[0, :, 3 * h:3 * h + 1])
            psum = psum + p
        p1, p2, p3 = _split3(psum)
        if seq_mode:
            imp_ref = scratch[0]
            ns = cov_ref.shape[0]
            imp = _nt(cov_ref[...], p1) + _nt(cov_ref[...], p2) + _nt(cov_ref[...], p3)
            sid = lax.broadcasted_iota(jnp.int32, (ns, tq), 0)
            tt = pos0 + qi * tq + lax.broadcasted_iota(jnp.int32, (ns, tq), 1)
            bt = tt // SEL_BLK
            forced = (sid == 0) | (sid == bt) | (sid == bt - 1)
            valid = sid * SEL_BLK <= tt
            imp_ref[...] = jnp.where(forced, SEL_BONUS, jnp.where(valid, imp, -SEL_BONUS))
            rank = _rank_rows(imp_ref, n_sel)
            sel_t = jnp.where(rank < k_top, 1.0, 0.0).astype(BF16)
            er = lax.broadcasted_iota(jnp.int32, (tq, tq), 0)
            ec = lax.broadcasted_iota(jnp.int32, (tq, tq), 1)
            eye = jnp.where(er == ec, 1.0, 0.0).astype(BF16)
            sel_ref[0, 0, g] = _nt(eye, sel_t).astype(BF16)
        else:
            ns = cov_ref.shape[1]
            imp = _dot(p1, cov_ref[...]) + _dot(p2, cov_ref[...]) + _dot(p3, cov_ref[...])
            sid = lax.broadcasted_iota(jnp.int32, (tq, ns), 1)
            tt = pos0 + qi * tq + lax.broadcasted_iota(jnp.int32, (tq, ns), 0)
            bt = tt // SEL_BLK
            forced = (sid == 0) | (sid == bt) | (sid == bt - 1)
            valid = sid * SEL_BLK <= tt
            sel_ref[0, 0, g:g + 1, :] = jnp.where(forced, SEL_BONUS, jnp.where(valid, imp, -SEL_BONUS))


def _cover(n_cmp_rows, n_sel_cols):
    cstart = np.arange(n_cmp_rows)[:, None] * CMP_STRIDE
    sstart = np.arange(n_sel_cols)[None, :] * SEL_BLK
    return ((cstart < sstart + SEL_BLK) & (cstart + CMP_BLK > sstart)).astype(np.float32)


def _cmp_attn_seq(q3, kcmp, vcmp, gate3, *, pos0=0):
    b, t, _ = q3.shape
    tq = min(QBLK, t)
    nc = kcmp.shape[1]
    n_cmp = (t - CMP_BLK) // CMP_STRIDE + 1
    n_sel = -(-t // SEL_BLK)
    k_top = min(N_SEL, n_sel)
    cov_t = jnp.asarray(_cover(nc, n_sel).T, BF16)
    gd = NSA_KV_HEADS * HEAD_DIM
    return pl.pallas_call(
        functools.partial(_cmpattn_kernel, tq=tq, n_cmp=n_cmp, n_sel=n_sel, k_top=k_top, pos0=pos0,
                          seq_mode=True),
        grid=(b, t // tq),
        in_specs=[pl.BlockSpec((1, tq, NSA_HEADS * HEAD_DIM), lambda i, j: (i, j, 0)),
                  pl.BlockSpec((1, nc, gd), lambda i, j: (i, 0, 0)),
                  pl.BlockSpec((1, nc, gd), lambda i, j: (i, 0, 0)),
                  pl.BlockSpec((1, tq, LANES), lambda i, j: (i, j, 0)),
                  pl.BlockSpec((n_sel, nc), lambda i, j: (0, 0))],
        out_specs=[pl.BlockSpec((1, tq, NSA_HEADS * HEAD_DIM), lambda i, j: (i, j, 0)),
                   pl.BlockSpec((1, 1, NSA_KV_HEADS, tq, n_sel), lambda i, j: (i, j, 0, 0, 0))],
        out_shape=[jax.ShapeDtypeStruct((b, t, NSA_HEADS * HEAD_DIM), F32),
                   jax.ShapeDtypeStruct((b, t // tq, NSA_KV_HEADS, tq, n_sel), BF16)],
        scratch_shapes=[pltpu.VMEM((n_sel, tq), F32)],
        compiler_params=_cparams("parallel", "parallel"), name="cmp_attn_seq")(q3, kcmp, vcmp, gate3, cov_t)


def _cmp_attn_step(q3, kcmp, vcmp, gate3, *, pos0, n_cmp, n_sel, ns_pad):
    b = q3.shape[0]
    nc = kcmp.shape[1]
    cov = jnp.asarray(_cover(nc, ns_pad), BF16)
    gd = NSA_KV_HEADS * HEAD_DIM
    return pl.pallas_call(
        functools.partial(_cmpattn_kernel, tq=1, n_cmp=n_cmp, n_sel=n_sel, k_top=0, pos0=pos0, seq_mode=False),
        grid=(b, 1),
        in_specs=[pl.BlockSpec((1, 1, NSA_HEADS * HEAD_DIM), lambda i, j: (i, 0, 0)),
                  pl.BlockSpec((1, nc, gd), lambda i, j: (i, 0, 0)),
                  pl.BlockSpec((1, nc, gd), lambda i, j: (i, 0, 0)),
                  pl.BlockSpec((1, 1, LANES), lambda i, j: (i, 0, 0)),
                  pl.BlockSpec((nc, ns_pad), lambda i, j: (0, 0))],
        out_specs=[pl.BlockSpec((1, 1, NSA_HEADS * HEAD_DIM), lambda i, j: (i, 0, 0)),
                   pl.BlockSpec((1, 1, NSA_KV_HEADS, ns_pad), lambda i, j: (i, 0, 0, 0))],
        out_shape=[jax.ShapeDtypeStruct((b, 1, NSA_HEADS * HEAD_DIM), F32),
                   jax.ShapeDtypeStruct((b, 1, NSA_KV_HEADS, ns_pad), F32)],
        compiler_params=_cparams("parallel", "arbitrary"), name="cmp_attn_step")(q3, kcmp, vcmp, gate3, cov)


def _selattn_kernel(q_ref, k_ref, v_ref, sel_ref, gate_ref, o_ref, qs_ref, m_ref, l_ref, acc_ref,
                    *, tq, tk, n_sel):
    qi = pl.program_id(1)
    hd = HEAD_DIM
    G, R = NSA_KV_HEADS, NSA_HEADS // NSA_KV_HEADS
    scale = hd ** -0.5
    n_tiles = ((qi + 1) * tq + tk - 1) // tk
    tt = qi * tq + lax.broadcasted_iota(jnp.int32, (tq, tk), 0)
    for g in range(G):
        for r in range(R):
            h = g * R + r
            qs_ref[r * tq:(r + 1) * tq, :] = (q_ref[0, :, h * hd:(h + 1) * hd] * scale).astype(BF16)
        m_ref[...] = jnp.full(m_ref.shape, NEG_INF, F32)
        l_ref[...] = jnp.zeros(l_ref.shape, F32)
        acc_ref[...] = jnp.zeros(acc_ref.shape, F32)
        selg = sel_ref[0, 0, g]

        def body(j, carry):
            k0 = pl.multiple_of(j * tk, tk)
            kt = k_ref[0, pl.ds(k0, tk), g * hd:(g + 1) * hd].astype(BF16)
            vt = v_ref[0, pl.ds(k0, tk), g * hd:(g + 1) * hd].astype(BF16)
            s = _nt(qs_ref[...], kt)
            blk = (k0 + lax.broadcasted_iota(jnp.int32, (n_sel, tk), 1)) // SEL_BLK
            expand = jnp.where(lax.broadcasted_iota(jnp.int32, (n_sel, tk), 0) == blk, 1.0, 0.0).astype(BF16)
            chosen = _dot(selg, expand)
            kpos = k0 + lax.broadcasted_iota(jnp.int32, (tq, tk), 1)
            neg = jnp.where((chosen > 0.5) & (kpos <= tt), 0.0, NEG_INF)
            s = s + jnp.concatenate([neg] * R, axis=0)
            m_old = m_ref[...]
            m_new = jnp.maximum(m_old, jnp.max(s, axis=-1, keepdims=True))
            m_safe = jnp.where(m_new == NEG_INF, 0.0, m_new)
            alpha = jnp.exp(m_old - m_safe)
            p = jnp.exp(s - m_safe)
            l_ref[...] = alpha * l_ref[...] + jnp.sum(p, axis=-1, keepdims=True)
            acc_ref[...] = alpha * acc_ref[...] + _dot(p.astype(BF16), vt)
            m_ref[...] = m_new
            return carry

        lax.fori_loop(0, n_tiles, body, 0)
        inv = 1.0 / jnp.maximum(l_ref[...], 1e-30)
        for r in range(R):
            h = g * R + r
            o = acc_ref[r * tq:(r + 1) * tq, :] * inv[r * tq:(r + 1) * tq]
            o_ref[0, :, h * hd:(h + 1) * hd] = o * _sigmoid(gate_ref[0, :, 3 * h + 1:3 * h + 2])


def _sel_attn_seq(q3, k3, v3, sel, gate3):
    b, t, _ = q3.shape
    tq = min(QBLK, t)
    tk = min(2 * QBLK, t)
    n_sel = sel.shape[-1]
    R = NSA_HEADS // NSA_KV_HEADS
    gd = NSA_KV_HEADS * HEAD_DIM
    return pl.pallas_call(
        functools.partial(_selattn_kernel, tq=tq, tk=tk, n_sel=n_sel),
        grid=(b, t // tq),
        in_specs=[pl.BlockSpec((1, tq, NSA_HEADS * HEAD_DIM), lambda i, j: (i, j, 0)),
                  pl.BlockSpec((1, t, gd), lambda i, j: (i, 0, 0)),
                  pl.BlockSpec((1, t, gd), lambda i, j: (i, 0, 0)),
                  pl.BlockSpec((1, 1, NSA_KV_HEADS, tq, n_sel), lambda i, j: (i, j, 0, 0, 0)),
                  pl.BlockSpec((1, tq, LANES), lambda i, j: (i, j, 0))],
        out_specs=pl.BlockSpec((1, tq, NSA_HEADS * HEAD_DIM), lambda i, j: (i, j, 0)),
        out_shape=jax.ShapeDtypeStruct((b, t, NSA_HEADS * HEAD_DIM), F32),
        scratch_shapes=[pltpu.VMEM((R * tq, HEAD_DIM), BF16), pltpu.VMEM((R * tq, 1), F32),
                        pltpu.VMEM((R * tq, 1), F32), pltpu.VMEM((R * tq, HEAD_DIM), F32)],
        compiler_params=_cparams("parallel", "parallel"), name="sel_attn_seq")(q3, k3, v3, sel, gate3)


def _selstep_kernel(*refs, g, n_slots):
    idx_ref, new_ref = refs[0], refs[1]
    q_ref = refs[2]
    k_refs = refs[3:3 + n_slots]
    v_refs = refs[3 + n_slots:3 + 2 * n_slots]
    nk_ref, nv_ref, gate_ref, o_ref = refs[3 + 2 * n_slots:]
    b = pl.program_id(0)
    hd = HEAD_DIM
    R = NSA_HEADS // NSA_KV_HEADS
    scale = hd ** -0.5
    qg = (q_ref[0, 0] * scale).astype(BF16)
    row = lax.broadcasted_iota(jnp.int32, (SEL_BLK, hd), 0)
    col = lax.broadcasted_iota(jnp.int32, (R, SEL_BLK), 1)
    new_mask = jnp.where(col > 0, NEG_INF, 0.0)
    nk =jnp.where(row == 0, nk_ref[0, :, g * hd:(g + 1) * hd], 0.0)
    nv = jnp.where(row == 0, nv_ref[0, :, g * hd:(g + 1) * hd], 0.0)
    scores, vals = [], []
    m = jnp.full((R, 1), NEG_INF, F32)
    for j in range(n_slots):
        is_new = new_ref[b * n_slots + j] == 1
        kj = jnp.where(is_new, nk, k_refs[j][0, :, g * hd:(g + 1) * hd])
        vj = jnp.where(is_new, nv, v_refs[j][0, :, g * hd:(g + 1) * hd])
        s = _nt(qg, kj.astype(BF16))
        s = s + jnp.where(is_new, new_mask, 0.0)
        m = jnp.maximum(m, jnp.max(s, axis=-1, keepdims=True))
        scores.append(s)
        vals.append(vj.astype(BF16))
    den = jnp.zeros((R, 1), F32)
    es = []
    for s in scores:
        e = jnp.exp(s - m)
        den = den + jnp.sum(e, axis=-1, keepdims=True)
        es.append(e)
    den = jnp.maximum(den, 1e-30)
    o = jnp.zeros((R, hd), F32)
    for e, vj in zip(es, vals):
        o = o + _dot((e / den).astype(BF16), vj)
    o_ref[0, 0] = o * _sigmoid(gate_ref[0, 0, :, 1:2])


def _sel_attn_step(q4, cache_k, cache_v, blk_idx, blk_new, new_k, new_v, gate4, g):
    b = q4.shape[0]
    n_pool, page, G, hd = cache_k.shape
    gd = G * hd
    R = NSA_HEADS // NSA_KV_HEADS
    per_page = page // SEL_BLK
    ck = cache_k.reshape(n_pool * per_page, SEL_BLK, gd)
    cv = cache_v.reshape(n_pool * per_page, SEL_BLK, gd)
    n_slots = N_SEL

    def slot_spec(j):
        return pl.BlockSpec((1, SEL_BLK, gd), lambda i, idx, new: (idx[i * n_slots + j], 0, 0))

    in_specs = ([pl.BlockSpec((1, 1, R, hd), lambda i, idx, new: (i, g, 0, 0))]
                + [slot_spec(j) for j in range(n_slots)] * 2
                + [pl.BlockSpec((1, 1, gd), lambda i, idx, new: (i, 0, 0))] * 2
                + [pl.BlockSpec((1, 1, R, 3), lambda i, idx, new: (i, g, 0, 0))])
    gs = pltpu.PrefetchScalarGridSpec(
        num_scalar_prefetch=2, grid=(b,), in_specs=in_specs,
        out_specs=pl.BlockSpec((1, 1, R, hd), lambda i, idx, new: (i, 0, 0, 0)))
    return pl.pallas_call(
        functools.partial(_selstep_kernel, g=g, n_slots=n_slots), grid_spec=gs,
        out_shape=jax.ShapeDtypeStruct((b, 1, R, hd), F32),
        compiler_params=_cparams("arbitrary"), name="sel_attn_step")(
            blk_idx, blk_new, q4, *([ck] * n_slots), *([cv] * n_slots),
            new_k.reshape(b, 1, gd), new_v.reshape(b, 1, gd), gate4)


def _mem_kernel(q_ref, k_ref, v_ref, g_ref, o_ref):
    hd = MEM_HEAD_DIM
    for h in range(MEM_HEADS):
        sl = slice(h * hd, (h + 1) * hd)
        qh = q_ref[0, :, sl]
        qn = qh * lax.rsqrt(jnp.mean(qh * qh, axis=-1, keepdims=True) + EPS) * g_ref[...]
        s = _nt(qn.astype(BF16), k_ref[0, :, sl].astype(BF16)) * (hd ** -0.5)
        m = jnp.max(s, axis=-1, keepdims=True)
        e = jnp.exp(s - m)
        p = e / jnp.sum(e, axis=-1, keepdims=True)
        o_ref[0, :, sl] = _dot(p.astype(BF16), v_ref[0, :, sl].astype(BF16))


def _mem_attn(q3, k3, v3, gq):
    b, t, wd = q3.shape
    mt = k3.shape[1]
    tm = _row_tile(t, 512)
    return pl.pallas_call(
        _mem_kernel, grid=(b, t // tm),
        in_specs=[pl.BlockSpec((1, tm, wd), lambda i, j: (i, j, 0)),
                  pl.BlockSpec((1, mt, wd), lambda i, j: (i, 0, 0)),
                  pl.BlockSpec((1, mt, wd), lambda i, j: (i, 0, 0)),
                  pl.BlockSpec((1, MEM_HEAD_DIM), lambda i, j: (0, 0))],
        out_specs=pl.BlockSpec((1, tm, wd), lambda i, j: (i, j, 0)),
        out_shape=jax.ShapeDtypeStruct((b, t, wd), F32),
        compiler_params=_cparams("parallel", "parallel"), name="mem_attn")(
            q3, k3, v3, gq.reshape(1, MEM_HEAD_DIM))


FFN_COLS = 256


def _ffn_kernel(*refs, ff, state_mode):
    if state_mode:
        u_ref, p2_ref, p1_ref, cw_ref, cb_ref, wo_ref, x_ref, o_ref, acc_ref = refs
    else:
        u_ref, halo_ref, cw_ref, cb_ref, wo_ref, x_ref, o_ref, acc_ref = refs
        first = pl.program_id(1) == 0

    def conv(off):
        sl = slice(off, off + FFN_COLS)
        if state_mode:
            cur, s1, s2 = u_ref[:, sl], p1_ref[:, sl], p2_ref[:, sl]
        else:
            cur = u_ref[0, :, sl]
            halo = jnp.where(first, 0.0, halo_ref[0, :, sl])
            s1 = _shift_rows(cur, halo, 1)
            s2 = _shift_rows(cur, halo, 2)
        return s2 * cw_ref[0:1, sl] + s1 * cw_ref[1:2, sl] + cur * cw_ref[2:3, sl] + cb_ref[:, sl]

    acc_ref[...] = x_ref[...] if state_mode else x_ref[0]
    for c in range(ff // FFN_COLS):
        a = conv(c * FFN_COLS)
        b = conv(ff + c * FFN_COLS)
        act = (_silu(a) * b).astype(BF16)
        acc_ref[...] += _dot(act, wo_ref[c * FFN_COLS:(c + 1) * FFN_COLS, :])
    if state_mode:
        o_ref[...] = acc_ref[...]
    else:
        o_ref[0] = acc_ref[...]


def _ffn_tail_seq(u3, cw8, cb, wo, x3):
    b, t, f2 = u3.shape
    ff = f2 // 2
    d = x3.shape[-1]
    assert ff % FFN_COLS == 0
    tm = _row_tile(t, 256)
    hb = tm // SUBLANES
    return pl.pallas_call(
        functools.partial(_ffn_kernel, ff=ff, state_mode=False),
        grid=(b, t // tm),
        in_specs=[pl.BlockSpec((1, tm, f2), lambda i, j: (i, j, 0)),
                  pl.BlockSpec((1, SUBLANES, f2), lambda i, j: (i, jnp.maximum(j * hb - 1, 0), 0)),
                  pl.BlockSpec((SUBLANES, f2), lambda i, j: (0, 0)),
                  pl.BlockSpec((1, f2), lambda i, j: (0, 0)),
                  pl.BlockSpec((ff, d), lambda i, j: (0, 0)),
                  pl.BlockSpec((1, tm, d), lambda i, j: (i, j, 0))],
        out_specs=pl.BlockSpec((1, tm, d), lambda i, j: (i, j, 0)),
        out_shape=jax.ShapeDtypeStruct((b, t, d), F32),
        scratch_shapes=[pltpu.VMEM((tm, d), F32)],
        compiler_params=_cparams("parallel", "arbitrary"), name="ffn_tail_seq")(u3, u3, cw8, cb, wo, x3)


def _ffn_tail_step(u, prev2, prev1, cw8, cb, wo, x):
    m, f2 = u.shape
    ff = f2 // 2
    d = x.shape[-1]
    tm = _row_tile(m, 128)
    row = pl.BlockSpec((tm, f2), lambda i: (i, 0))
    return pl.pallas_call(
        functools.partial(_ffn_kernel, ff=ff, state_mode=True),
        grid=(m // tm,),
        in_specs=[row, row, row,
                  pl.BlockSpec((SUBLANES, f2), lambda i: (0, 0)), pl.BlockSpec((1, f2), lambda i: (0, 0)),
                  pl.BlockSpec((ff, d), lambda i: (0, 0)), pl.BlockSpec((tm, d), lambda i: (i, 0))],
        out_specs=pl.BlockSpec((tm, d), lambda i: (i, 0)),
        out_shape=jax.ShapeDtypeStruct((m, d), F32),
        scratch_shapes=[pltpu.VMEM((tm, d), F32)],
        compiler_params=_cparams("parallel"), name="ffn_tail_step")(u, prev2, prev1, cw8, cb, wo, x)


def _pad_rows(a, rows):
    return jnp.pad(a.astype(F32), ((0, rows - a.shape[0]), (0, 0)))


def _pad_cols(w, cols):
    return jnp.pad(w, ((0, 0), (0, cols - w.shape[1])))


def _mem_block(x, bsz, t, mem_k3, mem_v3, P, layer):
    qm = _proj([x], P['mem_w_q'][layer], gain=P['mem_norm_x'][layer])
    om = _mem_attn(qm.reshape(bsz, t, -1), mem_k3, mem_v3, P['mem_q_norm'][layer])
    return _proj([om.reshape(bsz * t, -1)], P['mem_w_o'][layer], res=x)


def _ffn_in(x, P, layer):
    f2 = P['ffn_w_in'][layer].shape[1]
    tn = f2 // 4 if (f2 // 4) % LANES == 0 else f2
    return _proj([x], P['ffn_w_in'][layer], gain=P['ffn_norm'][layer], tn=tn)


def _dn_params(P):
    pa = jnp.zeros((1, LANES), F32).at[0, DN_HEADS:2 * DN_HEADS].set(P['l0_dn_a_log'].astype(F32))
    pd = jnp.zeros((1, LANES), F32).at[0, DN_HEADS:2 * DN_HEADS].set(P['l0_dn_dt_bias'].astype(F32))
    return _pad_rows(P['l0_dn_conv_w'], SUBLANES), pa, pd, P['l0_dn_norm'].astype(F32).reshape(1, DN_DK)


L0_SEG = (1536, 512, LANES, 512, 128, 128)
L1_SEG = (1024, 256, 256, 256, 256, 256, 256, LANES)


def _prompt_group(x_prompt, mem_prompt, P):
    bsz, t, d = x_prompt.shape
    m = bsz * t
    x = x_prompt.reshape(m, d)
    tabs = _rope_tables(jnp.arange(t, dtype=jnp.int32))

    mt = mem_prompt.shape[1]
    mem_rows = mem_prompt.reshape(bsz * mt, d)
    mem_k, mem_v = [], []
    for layer in range(2):
        kraw, v = _proj([mem_rows], P['mem_w_kv'][layer], gain=P['mem_norm_kv'][layer],
                        seg=(MEM_HEADS * MEM_HEAD_DIM,) * 2)
        mem_k.append(_headnorm(kraw, P['mem_k_norm'][layer], MEM_HEAD_DIM).reshape(bsz, mt, -1))
        mem_v.append(v.reshape(bsz, mt, -1))

    qkv, z, ba, q_b, k_b, v_b = _proj([x], P['l0_w_in'], gain=P['mix_norm'][0], seg=L0_SEG)
    cw8, pa, pd, gn = _dn_params(P)
    qkv3 = qkv.reshape(bsz, t, -1)
    w, u, qg, kd, qk, gc = _dn_prepare(qkv3, ba.reshape(bsz, t, LANES), cw8, pa, pd)
    o_dn, s_fin = _dn_recur(w, u, qg, kd, qk, gc, z.reshape(bsz, t, -1), gn)
    qs = _headnorm(q_b, P['l0_swa_q_norm'], HEAD_DIM, tabs=tabs, seq_len=t, want_norm=False)
    ks = _headnorm(k_b, P['l0_swa_k_norm'], HEAD_DIM, tabs=tabs, seq_len=t, want_norm=False)
    ks3 = ks.reshape(bsz, t, -1)
    vs3 = v_b.reshape(bsz, t, -1)
    o_sw = _band_attn(qs.reshape(bsz, t, -1), ks3, vs3, H=SWA_HEADS, G=SWA_KV_HEADS, window=SWA_WINDOW,
                      sinks=P['l0_swa_sinks'])
    mix = jnp.concatenate([o_dn, o_sw], axis=-1).reshape(m, -1)
    x = _proj([mix], P['l0_w_out'], res=x)
    x = _mem_block(x, bsz, t, mem_k[0], mem_v[0], P, 0)
    u0 = _ffn_in(x, P, 0)
    u03 = u0.reshape(bsz, t, -1)
    x = _ffn_tail_seq(u03, P['ffn_cw8'][0], P['ffn_cb'][0], P['ffn_w_out'][0], x.reshape(bsz, t, d)).reshape(m, d)
    nb = min(SWA_WINDOW, t)
    ab_state = (s_fin, qkv3[:, t - 3:], ks3[:, t - nb:].reshape(bsz, nb, SWA_KV_HEADS, HEAD_DIM),
                vs3[:, t - nb:].reshape(bsz, nb, SWA_KV_HEADS, HEAD_DIM))

    q, kc, vc, ksr, vsel, kwr, vw, gr = _proj([x], P['l1_w_in'], gain=P['mix_norm'][1], seg=L1_SEG)
    qn, qr = _headnorm(q, P['l1_q_norm'], HEAD_DIM, tabs=tabs, seq_len=t)
    ksel = _headnorm(ksr, P['l1_k_norm_sel'], HEAD_DIM, tabs=tabs, seq_len=t, want_norm=False)
    kw = _headnorm(kwr, P['l1_k_norm_win'], HEAD_DIM, tabs=tabs, seq_len=t, want_norm=False)
    gd = NSA_KV_HEADS * HEAD_DIM
    r3 = lambda a: a.reshape(bsz, t, -1)
    kcmp = _compress_seq(r3(kc), P['l1_cmp_pe_k'], P['l1_cmp_w1_k'], P['l1_cmp_w2_k'], P['l1_k_norm_cmp'])
    vcmp = _compress_seq(r3(vc), P['l1_cmp_pe_v'], P['l1_cmp_w1_v'], P['l1_cmp_w2_v'])
    gr3 = r3(gr)
    o_c, sel = _cmp_attn_seq(r3(qn), kcmp, vcmp, gr3)
    o_s = _sel_attn_seq(r3(qr), r3(ksel), r3(vsel), sel, gr3)
    o_w = _band_attn(r3(qr), r3(kw), r3(vw), H=NSA_HEADS, G=NSA_KV_HEADS, window=NSA_WINDOW,
                     gate3=gr3, gate_col=2)
    x = _proj([o_c.reshape(m, -1), o_s.reshape(m, -1), o_w.reshape(m, -1)], P['l1_w_out'], res=x)
    x = _mem_block(x, bsz, t, mem_k[1], mem_v[1], P, 1)
    u1 = _ffn_in(x, P, 1)
    u13 = u1.reshape(bsz, t, -1)
    x = _ffn_tail_seq(u13, P['ffn_cw8'][1], P['ffn_cb'][1], P['ffn_w_out'][1], x.reshape(bsz, t, d)).reshape(m, d)
    nw = min(NSA_WINDOW, t)
    r4 = lambda a: a.reshape(bsz, t, NSA_KV_HEADS, HEAD_DIM)
    nsa_state = (r4(kc), r4(vc), r4(ksel), r4(vsel), r4(kw)[:, t - nw:], r4(vw)[:, t - nw:])
    mem_kr = jnp.stack([k.reshape(bsz, mt, MEM_HEADS, MEM_HEAD_DIM) for k in mem_k])
    mem_vr = jnp.stack([v.reshape(bsz, mt, MEM_HEADS, MEM_HEAD_DIM) for v in mem_v])
    ffn_state = jnp.stack([u03[:, t - 2:], u13[:, t - 2:]])
    return x.reshape(bsz, t, d), ab_state, nsa_state, mem_kr, mem_vr, ffn_state


def _sample_group(x_sample, state_dn, state_dn_conv, cache_swa_k, cache_swa_v, cache_cmp_k, cache_cmp_v,
                  cache_sel_k, cache_sel_v, cache_win_k, cache_win_v, cache_mem_k, cache_mem_v,
                  state_ffn_conv, page_table, P):
    bsz, t, d = x_sample.shape
    assert t == 1
    page = cache_cmp_k.shape[1]
    n_pages = page_table.shape[1]
    past = n_pages * page
    assert cache_swa_k.shape[1] == SWA_WINDOW and cache_win_k.shape[1] == NSA_WINDOW
    x = x_sample.reshape(bsz, d)
    tabs = _rope_tables(jnp.full((1,), past, jnp.int32))
    mt = cache_mem_k.shape[2]

    qkv, z, ba, q_b, k_b, v_b = _proj([x], P['l0_w_in'], gain=P['mix_norm'][0], seg=L0_SEG)
    cw8, pa, pd, gn = _dn_params(P)
    o_dn, s_new, conv_new = _dn_step(qkv, state_dn_conv, ba, z, state_dn, cw8, pa, pd, gn)
    qs = _headnorm(q_b, P['l0_swa_q_norm'], HEAD_DIM, tabs=tabs, want_norm=False)
    ks = _headnorm(k_b, P['l0_swa_k_norm'], HEAD_DIM, tabs=tabs, want_norm=False)
    swa_k, swa_v, o_sw = _step_window_attn(
        qs.reshape(bsz, SWA_HEADS, HEAD_DIM), cache_swa_k.reshape(bsz, SWA_WINDOW, -1),
        cache_swa_v.reshape(bsz, SWA_WINDOW, -1), ks, v_b, H=SWA_HEADS, G=SWA_KV_HEADS,
        sinks=P['l0_swa_sinks'])
    mix = jnp.concatenate([o_dn.reshape(bsz, -1), o_sw.reshape(bsz, -1)], axis=-1)
    x = _proj([mix], P['l0_w_out'], res=x)
    x = _mem_block(x, bsz, 1, cache_mem_k[0].reshape(bsz, mt, -1), cache_mem_v[0].reshape(bsz, mt, -1), P, 0)
    u0 = _ffn_in(x, P, 0)
    x = _ffn_tail_step(u0, state_ffn_conv[0, :, 0], state_ffn_conv[0, :, 1], P['ffn_cw8'][0], P['ffn_cb'][0],
                       P['ffn_w_out'][0], x)
    ab_state = (s_new, conv_new, swa_k.reshape(bsz, SWA_WINDOW, SWA_KV_HEADS, HEAD_DIM),
                swa_v.reshape(bsz, SWA_WINDOW, SWA_KV_HEADS, HEAD_DIM))

    q, kc, vc, ksr, vsel, kwr, vw, gr = _proj([x], P['l1_w_in'], gain=P['mix_norm'][1], seg=L1_SEG)
    qn, qr = _headnorm(q, P['l1_q_norm'], HEAD_DIM, tabs=tabs)
    ksel = _headnorm(ksr, P['l1_k_norm_sel'], HEAD_DIM, tabs=tabs, want_norm=False)
    kw = _headnorm(kwr, P['l1_k_norm_win'], HEAD_DIM, tabs=tabs, want_norm=False)
    G, R = NSA_KV_HEADS, NSA_HEADS // NSA_KV_HEADS
    total = past + 1
    n_cmp = (total - CMP_BLK) // CMP_STRIDE + 1
    n_sel = -(-total // SEL_BLK)
    ns_pad = -(-n_sel // LANES) * LANES
    kcmp = _compress_paged(cache_cmp_k, page_table, P['l1_cmp_pe_k'], P['l1_cmp_w1_k'], P['l1_cmp_w2_k'],
                           P['l1_k_norm_cmp'])
    vcmp = _compress_paged(cache_cmp_v, page_table, P['l1_cmp_pe_v'], P['l1_cmp_w1_v'], P['l1_cmp_w2_v'])
    gr3 = gr.reshape(bsz, 1, LANES)
    o_c, imp = _cmp_attn_step(qn.reshape(bsz, 1, -1), kcmp, vcmp, gr3, pos0=past, n_cmp=n_cmp, n_sel=n_sel,
                              ns_pad=ns_pad)
    imp_t = imp.reshape(bsz * G, ns_pad).T
    rank = _rank_call(imp_t, n_sel)
    rank = jnp.where(jnp.arange(ns_pad)[:, None] < n_sel, rank, 1e9)
    blocks = jnp.argsort(rank, axis=0)[:N_SEL].T.astype(jnp.int32)
    blocks = blocks.reshape(bsz, G, N_SEL)
    per_page = page // SEL_BLK
    is_new = blocks >= n_pages * per_page
    blk_c = jnp.minimum(blocks, n_pages * per_page - 1)
    pages = jnp.take_along_axis(page_table[:, None, :], blk_c // per_page, axis=2)
    row_blk = pages * per_page + blk_c % per_page
    q4 = qr.reshape(bsz, G, R, HEAD_DIM)
    gate4 = gr[:, :3 * NSA_HEADS].reshape(bsz, G, R, 3)
    o_s = jnp.concatenate(
        [_sel_attn_step(q4, cache_sel_k, cache_sel_v, row_blk[:, g].reshape(-1),
                        is_new[:, g].astype(jnp.int32).reshape(-1), ksel, vsel, gate4, g) for g in range(G)],
        axis=1)
    win_k, win_v, o_w = _step_window_attn(
        qr.reshape(bsz, NSA_HEADS, HEAD_DIM), cache_win_k.reshape(bsz, NSA_WINDOW, -1),
        cache_win_v.reshape(bsz, NSA_WINDOW, -1), kw, vw, H=NSA_HEADS, G=NSA_KV_HEADS,
        gate=gr[:, :3 * NSA_HEADS].reshape(bsz, NSA_HEADS, 3), gate_col=2)
    x = _proj([o_c.reshape(bsz, -1), o_s.reshape(bsz, -1), o_w.reshape(bsz, -1)], P['l1_w_out'], res=x)
    x = _mem_block(x, bsz, 1, cache_mem_k[1].reshape(bsz, mt, -1), cache_mem_v[1].reshape(bsz, mt, -1), P, 1)
    u1 = _ffn_in(x, P, 1)
    x = _ffn_tail_step(u1, state_ffn_conv[1, :, 0], state_ffn_conv[1, :, 1], P['ffn_cw8'][1], P['ffn_cb'][1],
                       P['ffn_w_out'][1], x)
    r4 = lambda a: a.reshape(bsz, 1, G, HEAD_DIM)
    nsa_state = (r4(kc), r4(vc), r4(ksel), r4(vsel), win_k.reshape(bsz, NSA_WINDOW, G, HEAD_DIM),
                 win_v.reshape(bsz, NSA_WINDOW, G, HEAD_DIM))
    ffn_state = jnp.stack([jnp.stack([state_ffn_conv[0, :, 1], u0], axis=1),
                           jnp.stack([state_ffn_conv[1, :, 1], u1], axis=1)])
    return x.reshape(bsz, 1, d), ab_state, nsa_state, ffn_state


def kernel(x_prompt, x_sample, state_dn, state_dn_conv, cache_swa_k, cache_swa_v, cache_cmp_k, cache_cmp_v, cache_sel_k, cache_sel_v, cache_win_k, cache_win_v, cache_mem_k, cache_mem_v, state_ffn_conv, page_table, mem_prompt, mix_norm, l0_w_in, l0_dn_conv_w, l0_dn_a_log, l0_dn_dt_bias, l0_dn_norm, l0_swa_q_norm, l0_swa_k_norm, l0_swa_sinks, l0_w_out, l1_w_in, l1_q_norm, l1_k_norm_cmp, l1_k_norm_sel, l1_k_norm_win, l1_cmp_pe_k, l1_cmp_w1_k, l1_cmp_w2_k, l1_cmp_pe_v, l1_cmp_w1_v, l1_cmp_w2_v, l1_w_out, mem_norm_x, mem_norm_kv, mem_w_q, mem_w_k, mem_w_v, mem_q_norm, mem_k_norm, mem_w_o, ffn_norm, ffn_w_in, ffn_conv_w, ffn_conv_b, ffn_w_out):
    n_b = 2 * DN_HEADS
    c0 = 1536 + 512
    w0 = jnp.concatenate([l0_w_in[:, :c0], _pad_cols(l0_w_in[:, c0:c0 + n_b], LANES), l0_w_in[:, c0 + n_b:]], axis=1)
    c1 = (NSA_HEADS + 6 * NSA_KV_HEADS) * HEAD_DIM
    w1 = jnp.concatenate([l1_w_in[:, :c1], _pad_cols(l1_w_in[:, c1:], LANES)], axis=1)
    P = dict(
        mix_norm=mix_norm, l0_w_in=w0.astype(BF16), l0_dn_conv_w=l0_dn_conv_w, l0_dn_a_log=l0_dn_a_log,
        l0_dn_dt_bias=l0_dn_dt_bias, l0_dn_norm=l0_dn_norm, l0_swa_q_norm=l0_swa_q_norm,
        l0_swa_k_norm=l0_swa_k_norm, l0_swa_sinks=l0_swa_sinks, l0_w_out=l0_w_out.astype(BF16),
        l1_w_in=w1.astype(BF16), l1_q_norm=l1_q_norm, l1_k_norm_cmp=l1_k_norm_cmp, l1_k_norm_sel=l1_k_norm_sel,
        l1_k_norm_win=l1_k_norm_win, l1_cmp_pe_k=l1_cmp_pe_k, l1_cmp_w1_k=l1_cmp_w1_k, l1_cmp_w2_k=l1_cmp_w2_k,
        l1_cmp_pe_v=l1_cmp_pe_v, l1_cmp_w1_v=l1_cmp_w1_v, l1_cmp_w2_v=l1_cmp_w2_v,
        l1_w_out=l1_w_out.astype(BF16), mem_norm_x=mem_norm_x, mem_norm_kv=mem_norm_kv,
        mem_w_q=mem_w_q.astype(BF16), mem_w_kv=jnp.concatenate([mem_w_k, mem_w_v], axis=-1).astype(BF16),
        mem_q_norm=mem_q_norm, mem_k_norm=mem_k_norm, mem_w_o=mem_w_o.astype(BF16), ffn_norm=ffn_norm,
        ffn_w_in=ffn_w_in.astype(BF16),
        ffn_cw8=jnp.pad(ffn_conv_w.astype(F32), ((0, 0), (0, SUBLANES - ffn_conv_w.shape[1]), (0, 0))),
        ffn_cb=ffn_conv_b.astype(F32)[:, None, :], ffn_w_out=ffn_w_out.astype(BF16))

    y_p, ab_p, nsa_p, p_mem_k, p_mem_v, p_ffn = _prompt_group(x_prompt, mem_prompt, P)
    y_s, ab_s, nsa_s, s_ffn = _sample_group(
        x_sample, state_dn, state_dn_conv, cache_swa_k, cache_swa_v, cache_cmp_k, cache_cmp_v, cache_sel_k,
        cache_sel_v, cache_win_k, cache_win_v, cache_mem_k, cache_mem_v, state_ffn_conv, page_table, P)
    return (y_p, y_s, *ab_p, *nsa_p, p_mem_k, p_mem_v, p_ffn, *ab_s, *nsa_s, s_ffn)
```

```python
import functools

import numpy as np
import jax
import jax.numpy as jnp
from jax import lax
from jax.experimental import pallas as pl
from jax.experimental.pallas import tpu as pltpu

F32 = jnp.float32
BF16 = jnp.bfloat16
HI = lax.Precision.HIGHEST
NEG_INF = float("-inf")

EPS = 1e-6
HEAD_DIM = 64
ROT_DIM = HEAD_DIM // 4
ROPE_THETA = 500000.0
QBLK = 128
DN_HEADS = 4
DN_DK = 128
DN_CHUNK = 64
SWA_HEADS = 8
SWA_KV_HEADS = 2
SWA_WINDOW = 128
NSA_HEADS = 16
NSA_KV_HEADS = 4
CMP_BLK = 32
CMP_STRIDE = 16
SEL_BLK = 64
N_SEL = 16
NSA_WINDOW = 512
SEL_BONUS = 1e4
MEM_HEADS = 4
MEM_HEAD_DIM = 128

LANES = 128
SUBLANES = 8
VMEM_LIMIT = 56 << 20
PAGES_PER_STEP = 32


def _cparams(*sem):
    return pltpu.CompilerParams(dimension_semantics=sem, vmem_limit_bytes=VMEM_LIMIT)


def _nt(a, b, precision=None):
    return lax.dot_general(a, b, (((1,), (1,)), ((), ())), precision=precision,
                           preferred_element_type=F32)


def _tn(a, b, precision=None):
    return lax.dot_general(a, b, (((0,), (0,)), ((), ())), precision=precision,
                           preferred_element_type=F32)


def _dot(a, b, precision=None):
    return jnp.dot(a, b, precision=precision, preferred_element_type=F32)


def _hi_lo(a):
    hi = a.astype(BF16)
    return hi, (a - hi.astype(F32)).astype(BF16)


def _dot3(a, b):
    ah, al = _hi_lo(a)
    bh, bl = _hi_lo(b)
    return _dot(ah, bh) + _dot(ah, bl) + _dot(al, bh)


def _nt3(a, b):
    ah, al = _hi_lo(a)
    bh, bl = _hi_lo(b)
    return _nt(ah, bh) + _nt(ah, bl) + _nt(al, bh)


def _silu(x):
    return x / (1.0 + jnp.exp(-x))


def _sigmoid(x):
    return 1.0 / (1.0 + jnp.exp(-x))


def _softplus(x):
    return jnp.maximum(x, 0.0) + jnp.log(1.0 + jnp.exp(-jnp.abs(x)))


def _split3(x):
    a = x.astype(BF16)
    r = x - a.astype(F32)
    b = r.astype(BF16)
    c = (r - b.astype(F32)).astype(BF16)
    return a, b, c


def _row_tile(m, pref):
    for t in (1024, 512, 256, 128, 64, 32, 16, 8):
        if t <= pref and m % t == 0:
            return t
    return m


def _shift_rows(cur, halo, k):
    r = pltpu.roll(cur, k, 0)
    h = pltpu.roll(halo, k, 0)
    row = lax.broadcasted_iota(jnp.int32, h.shape, 0)
    top = jnp.where(row < k, h, r[0:SUBLANES])
    if cur.shape[0] == SUBLANES:
        return top
    return jnp.concatenate([top, r[SUBLANES:]], axis=0)


def _proj_kernel(*refs, n_add, has_gain, has_res, seg):
    it = iter(refs)
    adds = [next(it) for _ in range(n_add)]
    gain = next(it) if has_gain else None
    w = next(it)
    res = next(it) if has_res else None
    outs = [next(it) for _ in seg]
    xn = next(it)

    @pl.when(pl.program_id(1) == 0)
    def _():
        x = adds[0][...]
        for a in adds[1:]:
            x = x + a[...]
        if has_gain:
            ms = jnp.mean(x * x, axis=-1, keepdims=True)
            x = x * lax.rsqrt(ms + EPS) * gain[...]
        xn[...] = x.astype(BF16)

    acc = _dot(xn[...], w[...])
    if has_res:
        acc = acc + res[...]
    if len(seg) == 1:
        outs[0][...] = acc
    else:
        off = 0
        for o, s in zip(outs, seg):
            o[...] = acc[:, off:off + s]
            off += s


def _proj(adds, w, *, gain=None, res=None, seg=None, tn=None, tm_pref=512):
    m, k = adds[0].shape
    n = w.shape[1]
    tm = _row_tile(m, tm_pref)
    tn = n if tn is None else tn
    seg = (n,) if seg is None else tuple(seg)
    assert n % tn == 0 and (len(seg) == 1 or tn == n) and sum(seg) == n
    in_specs = [pl.BlockSpec((tm, k), lambda i, j: (i, 0)) for _ in adds]
    args = list(adds)
    if gain is not None:
        in_specs.append(pl.BlockSpec((1, k), lambda i, j: (0, 0)))
        args.append(gain.reshape(1, k))
    in_specs.append(pl.BlockSpec((k, tn), lambda i, j: (0, j)))
    args.append(w)
    if res is not None:
        in_specs.append(pl.BlockSpec((tm, tn), lambda i, j: (i, j)))
        args.append(res)
    if len(seg) == 1:
        out_shape = [jax.ShapeDtypeStruct((m, n), F32)]
        out_specs = [pl.BlockSpec((tm, tn), lambda i, j: (i, j))]
    else:
        out_shape = [jax.ShapeDtypeStruct((m, s), F32) for s in seg]
        out_specs = [pl.BlockSpec((tm, s), lambda i, j: (i, 0)) for s in seg]
    outs = pl.pallas_call(
        functools.partial(_proj_kernel, n_add=len(adds), has_gain=gain is not None,
                          has_res=res is not None, seg=seg),
        grid=(m // tm, n // tn), in_specs=in_specs, out_specs=out_specs, out_shape=out_shape,
        scratch_shapes=[pltpu.VMEM((tm, k), BF16)],
        compiler_params=_cparams("parallel", "arbitrary"), name="proj")(*args)
    return outs[0] if len(seg) == 1 else outs


def _hn_kernel(*refs, hd, want_norm, want_rope):
    x_ref, g_ref = refs[0], refs[1]
    i = 2
    if want_rope:
        c_ref, sa_ref, sb_ref = refs[i:i + 3]
        i += 3
    outs = refs[i:]
    width = x_ref.shape[1]
    r = lax.broadcasted_iota(jnp.int32, (LANES, LANES), 0) // hd
    c = lax.broadcasted_iota(jnp.int32, (LANES, LANES), 1) // hd
    bd = jnp.where(r == c, 1.0, 0.0).astype(BF16)
    for cb in range(width // LANES):
        sl = slice(cb * LANES, (cb + 1) * LANES)
        xc = x_ref[:, sl]
        x2 = xc * xc
        hi = x2.astype(BF16)
        lo = (x2 - hi.astype(F32)).astype(BF16)
        ss = _dot(hi, bd) + _dot(lo, bd)
        xc = xc * lax.rsqrt(ss * (1.0 / hd) + EPS) * g_ref[:, sl]
        k = 0
        if want_norm:
            outs[k][:, sl] = xc
            k += 1
        if want_rope:
            outs[k][:, sl] = (xc * c_ref[...] + pltpu.roll(xc, LANES - ROT_DIM // 2, 1) * sa_ref[...]
                              + pltpu.roll(xc, ROT_DIM // 2, 1) * sb_ref[...])


def _rope_tables(pos):
    half = ROT_DIM // 2
    inv = ROPE_THETA ** (-jnp.arange(half, dtype=F32) / half)
    ang = pos.astype(F32)[:, None] * inv[None, :]
    jj = np.arange(LANES) % HEAD_DIM
    cos = jnp.cos(ang)[:, jj % half]
    sin = jnp.sin(ang)[:, jj % half]
    c = jnp.where(jj[None, :] < ROT_DIM, cos, 1.0)
    sa = jnp.where(jj[None, :] < half, -sin, 0.0)
    sb = jnp.where((jj[None, :] >= half) & (jj[None, :] < ROT_DIM), sin, 0.0)
    return c, sa, sb


def _headnorm(x, gain, hd, *, tabs=None, seq_len=None, want_norm=True):
    m, width = x.shape
    want_rope = tabs is not None
    tm = _row_tile(seq_len if (want_rope and tabs[0].shape[0] > 1) else m, 512)
    g = jnp.tile(gain.reshape(1, hd), (1, width // hd))
    in_specs = [pl.BlockSpec((tm, width), lambda i: (i, 0)), pl.BlockSpec((1, width), lambda i: (0, 0))]
    args = [x, g]
    if want_rope:
        if tabs[0].shape[0] > 1:
            nt = seq_len // tm
            tspec = pl.BlockSpec((tm, LANES), lambda i: (i % nt, 0))
        else:
            tspec = pl.BlockSpec((1, LANES), lambda i: (0, 0))
        in_specs += [tspec] * 3
        args += list(tabs)
    n_out = int(want_norm) + int(want_rope)
    outs = pl.pallas_call(
        functools.partial(_hn_kernel, hd=hd, want_norm=want_norm, want_rope=want_rope),
        grid=(m // tm,), in_specs=in_specs,
        out_specs=[pl.BlockSpec((tm, width), lambda i: (i, 0))] * n_out,
        out_shape=[jax.ShapeDtypeStruct((m, width), F32)] * n_out,
        compiler_params=_cparams("parallel"), name="headnorm")(*args)
    return outs[0] if n_out == 1 else outs


def _dn_gates(ba, pa_ref, pd_ref):
    beta = _sigmoid(ba)
    g = -jnp.exp(pa_ref[...]) * _softplus(ba + pd_ref[...])
    return beta, g


def _dna_kernel(qkv_ref, halo_ref, ba_ref, cw_ref, pa_ref, pd_ref,
                w_o, u_o, qg_o, kd_o, qk_o, gc_o, *, C):
    n = pl.program_id(1)
    x = qkv_ref[0]
    halo = jnp.where(n == 0, 0.0, halo_ref[0])
    c = (x * cw_ref[3:4, :] + _shift_rows(x, halo, 1) * cw_ref[2:3, :]
         + _shift_rows(x, halo, 2) * cw_ref[1:2, :] + _shift_rows(x, halo, 3) * cw_ref[0:1, :])
    c = _silu(c)
    beta_all, gmat = _dn_gates(ba_ref[0], pa_ref, pd_ref)
    H, dk = DN_HEADS, DN_DK
    HC = H * C
    ri = lax.broadcasted_iota(jnp.int32, (C, C), 0)
    ci = lax.broadcasted_iota(jnp.int32, (C, C), 1)
    gc = _dot(jnp.where(ri >= ci, 1.0, 0.0), gmat, HI)
    gc_o[0] = gc
    qs, ks, vs, bs, gs = [], [], [], [], []
    for h in range(H):
        qh = c[:, h * dk:(h + 1) * dk]
        kh = c[:, (H + h) * dk:(H + h + 1) * dk]
        qs.append(qh * lax.rsqrt(jnp.sum(qh * qh, axis=-1, keepdims=True) + EPS) * (dk ** -0.5))
        ks.append(kh * lax.rsqrt(jnp.sum(kh * kh, axis=-1, keepdims=True) + EPS))
        vs.append(c[:, (2 * H + h) * dk:(2 * H + h + 1) * dk])
        bs.append(beta_all[:, h:h + 1])
        gs.append(gc[:, H + h:H + h + 1])
    qst = jnp.concatenate(qs, axis=0)
    kst = jnp.concatenate(ks, axis=0)
    vst = jnp.concatenate(vs, axis=0)
    bst = jnp.concatenate(bs, axis=0)
    gcol = jnp.concatenate(gs, axis=0)
    lane0 = lax.broadcasted_iota(jnp.int32, (HC, LANES), 1) == 0
    e0 = jnp.where((lax.broadcasted_iota(jnp.int32, (SUBLANES, LANES), 0) == 0)
                   & (lax.broadcasted_iota(jnp.int32, (SUBLANES, LANES), 1) == 0), 1.0, 0.0)
    grow = _nt(e0, jnp.where(lane0, gcol, 0.0), HI)[0:1]
    rr = lax.broadcasted_iota(jnp.int32, (HC, HC), 0)
    cc = lax.broadcasted_iota(jnp.int32, (HC, HC), 1)
    same = (rr // C) == (cc // C)
    lower = same & (rr >= cc)
    decay = jnp.where(lower, jnp.exp(jnp.where(lower, gcol - grow, 0.0)), 0.0)
    kbst = kst * bst
    lmat = jnp.where(same & (rr > cc), _nt3(kbst, kst) * decay, 0.0)
    tinv = jnp.where(rr == cc, 1.0, 0.0) - lmat
    n_fac = max(int(np.ceil(np.log2(C))) - 1, 0)
    if n_fac > 0:
        p = _dot3(lmat, lmat)
        for f in range(n_fac):
            tinv = tinv + _dot3(tinv, p)
            if f + 1 < n_fac:
                p = _dot3(p, p)
    eg = jnp.exp(gcol)
    wu = _dot3(tinv, jnp.concatenate([kbst * eg, vst * bst], axis=1))
    qk = jnp.where(lower, _nt3(qst, kst) * decay, 0.0)
    qk_sum = qk[0:C]
    for h in range(1, H):
        qk_sum = qk_sum + qk[h * C:(h + 1) * C]
    qk_o[0] = qk_sum
    for h in range(H):
        sl = slice(h * dk, (h + 1) * dk)
        rows = slice(h * C, (h + 1) * C)
        w_o[0, :, sl] = wu[rows, :dk]
        u_o[0, :, sl] = wu[rows, dk:]
        qg_o[0, :, sl] = qs[h] * eg[rows]
        kd_o[0, :, sl] = ks[h] * jnp.exp(gs[h][C - 1:C, :] - gs[h])


def _dn_prepare(qkv3, ba3, cw8, pa, pd):
    b, t, wq = qkv3.shape
    C = DN_CHUNK
    assert t % C == 0
    nh = DN_HEADS * DN_DK
    hb = C // SUBLANES
    outs = pl.pallas_call(
        functools.partial(_dna_kernel, C=C),
        grid=(b, t // C),
        in_specs=[pl.BlockSpec((1, C, wq), lambda i, n: (i, n, 0)),
                  pl.BlockSpec((1, SUBLANES, wq), lambda i, n: (i, jnp.maximum(n * hb - 1, 0), 0)),
                  pl.BlockSpec((1, C, LANES), lambda i, n: (i, n, 0)),
                  pl.BlockSpec((SUBLANES, wq), lambda i, n: (0, 0)),
                  pl.BlockSpec((1, LANES), lambda i, n: (0, 0)),
                  pl.BlockSpec((1, LANES), lambda i, n: (0, 0))],
        out_specs=[pl.BlockSpec((1, C, nh), lambda i, n: (i, n, 0))] * 4
        + [pl.BlockSpec((1, C, DN_HEADS * C), lambda i, n: (i, n, 0)),
           pl.BlockSpec((1, C, LANES), lambda i, n: (i, n, 0))],
        out_shape=[jax.ShapeDtypeStruct((b, t, nh), F32)] * 4
        + [jax.ShapeDtypeStruct((b, t, DN_HEADS * C), F32), jax.ShapeDtypeStruct((b, t, LANES), F32)],
        compiler_params=_cparams("parallel", "parallel"), name="dn_prepare")(qkv3, qkv3, ba3, cw8, pa, pd)
    return outs


def _dnb_kernel(w_ref, u_ref, qg_ref, kd_ref, qk_ref, gc_ref, z_ref, gn_ref, o_ref, s_out, s_ref, *, C, nb):
    n = pl.program_id(0)

    @pl.when(n == 0)
    def _():
        s_ref[...] = jnp.zeros_like(s_ref)

    dk = DN_DK
    for b in range(nb):
        for h in range(DN_HEADS):
            sl = slice(h * dk, (h + 1) * dk)
            s = s_ref[b * DN_HEADS + h]
            v_new = u_ref[b, :, sl] - _dot(w_ref[b, :, sl], s, HI)
            o = _dot(qg_ref[b, :, sl], s, HI) + _dot(qk_ref[b, :, h * C:(h + 1) * C], v_new, HI)
            dl = jnp.exp(gc_ref[b, C - 1:C, DN_HEADS + h:DN_HEADS + h + 1])
            s_ref[b * DN_HEADS + h] = s * dl + _tn(kd_ref[b, :, sl], v_new, HI)
            o = o * lax.rsqrt(jnp.mean(o * o, axis=-1, keepdims=True) + EPS) * gn_ref[...]
            o_ref[b, :, sl] = o * _silu(z_ref[b, :, sl])

    @pl.when(n == pl.num_programs(0) - 1)
    def _():
        for b in range(nb):
            for h in range(DN_HEADS):
                s_out[b, h] = s_ref[b * DN_HEADS + h]


def _dn_recur(w, u, qg, kd, qk, gc, z3, gn):
    b, t, nh = w.shape
    C = DN_CHUNK
    big = pl.BlockSpec((b, C, nh), lambda n: (0, n, 0))
    return pl.pallas_call(
        functools.partial(_dnb_kernel, C=C, nb=b),
        grid=(t // C,),
        in_specs=[big, big, big, big,
                  pl.BlockSpec((b, C, DN_HEADS * C), lambda n: (0, n, 0)),
                  pl.BlockSpec((b, C, LANES), lambda n: (0, n, 0)),
                  big, pl.BlockSpec((1, DN_DK), lambda n: (0, 0))],
        out_specs=[big, pl.BlockSpec((b, DN_HEADS, DN_DK, DN_DK), lambda n: (0, 0, 0, 0))],
        out_shape=[jax.ShapeDtypeStruct((b, t, nh), F32),
                   jax.ShapeDtypeStruct((b, DN_HEADS, DN_DK, DN_DK), F32)],
        scratch_shapes=[pltpu.VMEM((b * DN_HEADS, DN_DK, DN_DK), F32)],
        compiler_params=_cparams("arbitrary"), name="dn_recur")(w, u, qg, kd, qk, gc, z3, gn)


def _dns_kernel(qkv_ref, st_ref, ba_ref, z_ref, s_ref, cw_ref, pa_ref, pd_ref, gn_ref,
                o_ref, so_ref, sto_ref):
    new = qkv_ref[0]
    st = st_ref[0]
    c = (st[0:1] * cw_ref[0:1, :] + st[1:2] * cw_ref[1:2, :] + st[2:3] * cw_ref[2:3, :]
         + new * cw_ref[3:4, :])
    c = _silu(c)
    sto_ref[0, 0:2, :] = st[1:3]
    sto_ref[0, 2:3, :] = new
    beta_all, gmat = _dn_gates(ba_ref[0], pa_ref, pd_ref)
    dk = DN_DK
    row = lax.broadcasted_iota(jnp.int32, (SUBLANES, dk), 0)
    for h in range(DN_HEADS):
        qh = c[:, h * dk:(h + 1) * dk]
        kh = c[:, (DN_HEADS + h) * dk:(DN_HEADS + h + 1) * dk]
        vh = c[:, (2 * DN_HEADS + h) * dk:(2 * DN_HEADS + h + 1) * dk]
        qh = qh * lax.rsqrt(jnp.sum(qh * qh, axis=-1, keepdims=True) + EPS) * (dk ** -0.5)
        kh = kh * lax.rsqrt(jnp.sum(kh * kh, axis=-1, keepdims=True) + EPS)
        bh = beta_all[:, h:h + 1]
        eg = jnp.exp(gmat[:, DN_HEADS + h:DN_HEADS + h + 1])
        s = s_ref[0, h]
        kq = jnp.where(row == 0, kh, jnp.where(row == 1, qh, 0.0))
        ks_qs = _dot(kq, s, HI)
        v_new = bh * (vh - eg * ks_qs[0:1])
        o = eg * ks_qs[1:2] + jnp.sum(qh * kh, axis=-1, keepdims=True) * v_new
        k8 = jnp.where(row == 0, kh, 0.0)
        v8 = jnp.where(row == 0, v_new, 0.0)
        so_ref[0, h] = s * eg + _tn(k8, v8, HI)
        o = o * lax.rsqrt(jnp.mean(o * o, axis=-1, keepdims=True) + EPS) * gn_ref[...]
        o_ref[0, :, h * dk:(h + 1) * dk] = o * _silu(z_ref[0, :, h * dk:(h + 1) * dk])


def _dn_step(qkv, conv_state, ba, z, state, cw8, pa, pd, gn):
    b, wq = qkv.shape
    nh = DN_HEADS * DN_DK
    one = lambda i: (i, 0, 0)
    cst = lambda i: (0, 0)
    return pl.pallas_call(
        _dns_kernel, grid=(b,),
        in_specs=[pl.BlockSpec((1, 1, wq), one), pl.BlockSpec((1, 3, wq), one),
                  pl.BlockSpec((1, 1, LANES), one), pl.BlockSpec((1, 1, nh), one),
                  pl.BlockSpec((1, DN_HEADS, DN_DK, DN_DK), lambda i: (i, 0, 0, 0)),
                  pl.BlockSpec((SUBLANES, wq), cst), pl.BlockSpec((1, LANES), cst),
                  pl.BlockSpec((1, LANES), cst), pl.BlockSpec((1, DN_DK), cst)],
        out_specs=[pl.BlockSpec((1, 1, nh), one),
                   pl.BlockSpec((1, DN_HEADS, DN_DK, DN_DK), lambda i: (i, 0, 0, 0)),
                   pl.BlockSpec((1, 3, wq), one)],
        out_shape=[jax.ShapeDtypeStruct((b, 1, nh), F32),
                   jax.ShapeDtypeStruct(state.shape, F32),
                   jax.ShapeDtypeStruct(conv_state.shape, F32)],
        compiler_params=_cparams("parallel"), name="dn_step")(
            qkv.reshape(b, 1, wq), conv_state, ba.reshape(b, 1, LANES), z.reshape(b, 1, nh),
            state, cw8, pa, pd, gn)


def _band_kernel(*refs, H, G, window, tq, span, T, has_sink, gate_col):
    q_ref, k_ref, v_ref = refs[:3]
    i = 3
    sink_ref = gate_ref = None
    if has_sink:
        sink_ref = refs[i]
        i += 1
    if gate_col is not None:
        gate_ref = refs[i]
        i += 1
    o_ref = refs[i]
    qi = pl.program_id(1)
    start = jnp.minimum(jnp.maximum(qi * tq - window, 0), T - span)
    start = pl.multiple_of(start, tq)
    kb = k_ref[0, pl.ds(start, span), :]
    vb = v_ref[0, pl.ds(start, span), :]
    R = H // G
    hd = HEAD_DIM
    scale = hd ** -0.5
    qpos = qi * tq + lax.broadcasted_iota(jnp.int32, (span, tq), 1)
    kpos = start + lax.broadcasted_iota(jnp.int32, (span, tq), 0)
    d = qpos - kpos
    neg = jnp.where((d >= 0) & (d < window), 0.0, NEG_INF)
    neg = jnp.concatenate([neg] * R, axis=1)
    pad = jnp.zeros((LANES - hd, tq), F32)
    for g in range(G):
        kg = kb[:, g * hd:(g + 1) * hd].astype(BF16)
        vg = vb[:, g * hd:(g + 1) * hd].astype(BF16)
        qs = jnp.concatenate([(q_ref[0, :, (g * R + r) * hd:(g * R + r + 1) * hd] * scale).astype(BF16)
                              for r in range(R)], axis=0)
        st = _nt(kg, qs) + neg
        m = jnp.max(st, axis=0, keepdims=True)
        if has_sink:
            sk = jnp.concatenate([jnp.broadcast_to(sink_ref[0:1, g * R + r:g * R + r + 1], (1, tq))
                                  for r in range(R)], axis=1)
            m = jnp.maximum(m, sk)
            e = jnp.exp(st - m)
            den = jnp.sum(e, axis=0, keepdims=True) + jnp.exp(sk - m)
        else:
            m = jnp.where(m == NEG_INF, 0.0, m)
            e = jnp.exp(st - m)
            den = jnp.maximum(jnp.sum(e, axis=0, keepdims=True), 1e-30)
        acc = _tn(vg, e.astype(BF16)) / den
        for r in range(R):
            h = g * R + r
            o = jnp.concatenate([acc[:, r * tq:(r + 1) * tq], pad], axis=0).T[:, :hd]
            if gate_ref is not None:
                o = o * _sigmoid(gate_ref[0, :, 3 * h + gate_col:3 * h + gate_col + 1])
            o_ref[0, :, h * hd:(h + 1) * hd] = o


def _band_attn(q3, k3, v3, *, H, G, window, sinks=None, gate3=None, gate_col=None):
    b, t, _ = q3.shape
    tq = min(QBLK, t)
    assert t % tq == 0
    span = min(window + tq, t)
    in_specs = [pl.BlockSpec((1, tq, H * HEAD_DIM), lambda i, j: (i, j, 0)),
                pl.BlockSpec((1, t, G * HEAD_DIM), lambda i, j: (i, 0, 0)),
                pl.BlockSpec((1, t, G * HEAD_DIM), lambda i, j: (i, 0, 0))]
    args = [q3, k3, v3]
    if sinks is not None:
        in_specs.append(pl.BlockSpec((1, LANES), lambda i, j: (0, 0)))
        args.append(jnp.pad(sinks.astype(F32), (0, LANES - H)).reshape(1, LANES))
    if gate3 is not None:
        in_specs.append(pl.BlockSpec((1, tq, LANES), lambda i, j: (i, j, 0)))
        args.append(gate3)
    else:
        gate_col = None
    return pl.pallas_call(
        functools.partial(_band_kernel, H=H, G=G, window=window, tq=tq, span=span, T=t,
                          has_sink=sinks is not None, gate_col=gate_col),
        grid=(b, t // tq), in_specs=in_specs,
        out_specs=pl.BlockSpec((1, tq, H * HEAD_DIM), lambda i, j: (i, j, 0)),
        out_shape=jax.ShapeDtypeStruct((b, t, H * HEAD_DIM), F32),
        compiler_params=_cparams("parallel", "parallel"), name="band_attn")(*args)


def _swin_kernel(*refs, H, G, W, has_sink, gate_col):
    q_ref, ck_ref, cv_ref, nk_ref, nv_ref = refs[:5]
    i = 5
    sink_ref = gate_ref = None
    if has_sink:
        sink_ref = refs[i]
        i += 1
    if gate_col is not None:
        gate_ref = refs[i]
        i += 1
    ok_ref, ov_ref, o_ref = refs[i:i + 3]
    row = lax.broadcasted_iota(jnp.int32, (W, G * HEAD_DIM), 0)
    kn = jnp.where(row == W - 1, nk_ref[0], pltpu.roll(ck_ref[0], W - 1, 0))
    vn = jnp.where(row == W - 1, nv_ref[0], pltpu.roll(cv_ref[0], W - 1, 0))
    ok_ref[0] = kn
    ov_ref[0] = vn
    R = H // G
    hd = HEAD_DIM
    scale = hd ** -0.5
    for g in range(G):
        kg = kn[:, g * hd:(g + 1) * hd].astype(BF16)
        vg = vn[:, g * hd:(g + 1) * hd].astype(BF16)
        qg = (q_ref[0, g * R:(g + 1) * R, :] * scale).astype(BF16)
        s = _nt(qg, kg)
        m = jnp.max(s, axis=-1, keepdims=True)
        if has_sink:
            sk = sink_ref[g * R:(g + 1) * R, :]
            m = jnp.maximum(m, sk)
            e = jnp.exp(s - m)
            den = jnp.sum(e, axis=-1, keepdims=True) + jnp.exp(sk - m)
        else:
            e = jnp.exp(s - m)
            den = jnp.maximum(jnp.sum(e, axis=-1, keepdims=True), 1e-30)
        o = _dot((e / den).astype(BF16), vg)
        if gate_ref is not None:
            o = o * _sigmoid(gate_ref[0, g * R:(g + 1) * R, gate_col:gate_col + 1])
        o_ref[0, g * R:(g + 1) * R, :] = o


def _step_window_attn(q, cache_k, cache_v, new_k, new_v, *, H, G, sinks=None, gate=None, gate_col=None):
    b, W, gd = cache_k.shape
    one = lambda i: (i, 0, 0)
    in_specs = [pl.BlockSpec((1, H, HEAD_DIM), one), pl.BlockSpec((1, W, gd), one),
                pl.BlockSpec((1, W, gd), one), pl.BlockSpec((1, 1, gd), one), pl.BlockSpec((1, 1, gd), one)]
    args = [q, cache_k, cache_v, new_k.reshape(b, 1, gd), new_v.reshape(b, 1, gd)]
    if sinks is not None:
        in_specs.append(pl.BlockSpec((H, 1), lambda i: (0, 0)))
        args.append(sinks.astype(F32).reshape(H, 1))
    if gate is not None:
        in_specs.append(pl.BlockSpec((1, H, 3), one))
        args.append(gate)
    else:
        gate_col = None
    return pl.pallas_call(
        functools.partial(_swin_kernel, H=H, G=G, W=W, has_sink=sinks is not None, gate_col=gate_col),
        grid=(b,), in_specs=in_specs,
        out_specs=[pl.BlockSpec((1, W, gd), one), pl.BlockSpec((1, W, gd), one),
                   pl.BlockSpec((1, H, HEAD_DIM), one)],
        out_shape=[jax.ShapeDtypeStruct((b, W, gd), F32), jax.ShapeDtypeStruct((b, W, gd), F32),
                   jax.ShapeDtypeStruct((b, H, HEAD_DIM), F32)],
        compiler_params=_cparams("parallel"), name="step_window_attn")(*args)


def _cmp_core(load_rows, pe_ref, wp_ref, w2_ref, gain_ref, out_ref, *, r0, rd):
    lane = lax.broadcasted_iota(jnp.int32, (1, LANES), 1)
    out = jnp.zeros((r0, NSA_KV_HEADS * HEAD_DIM), F32)
    for gp in range(2):
        acc = jnp.zeros((rd + SUBLANES, 2 * LANES), F32)
        for ip in range(CMP_STRIDE // 2):
            ab = jnp.concatenate([load_rows(2 * ip, gp), load_rows(2 * ip + 1, gp)], axis=1)
            lhs = jnp.concatenate([ab, pe_ref[ip]], axis=0).astype(BF16)
            acc = acc + _dot(lhs, wp_ref[ip])
        for gl in range(2):
            piece = acc[:, gl * LANES:(gl + 1) * LANES]
            bias = jnp.where(lane < HEAD_DIM, piece[rd:rd + 1], piece[rd + 1:rd + 2])
            data = piece[:rd] + bias
            pre = data + pltpu.roll(pltpu.roll(data, rd - 1, 0), HEAD_DIM, 1)
            hcat = _silu(pre)[:r0]
            out = out + _dot(hcat.astype(BF16), w2_ref[2 * gp + gl])
    if gain_ref is not None:
        n = NSA_KV_HEADS * HEAD_DIM
        r = lax.broadcasted_iota(jnp.int32, (n, n), 0) // HEAD_DIM
        c = lax.broadcasted_iota(jnp.int32, (n, n), 1) // HEAD_DIM
        bd = jnp.where(r == c, 1.0, 0.0).astype(BF16)
        x2 = out * out
        hi = x2.astype(BF16)
        lo = (x2 - hi.astype(F32)).astype(BF16)
        ss = _dot(hi, bd) + _dot(lo, bd)
        out = out * lax.rsqrt(ss * (1.0 / HEAD_DIM) + EPS) * gain_ref[...]
    out_ref[0] = out


def _cmp_seq_kernel(*refs, r0, has_norm):
    x_refs, (pe_ref, wp_ref, w2_ref) = refs[:2], refs[2:5]
    gain_ref = refs[5] if has_norm else None
    out_ref = refs[-1]

    def load_rows(i, gp):
        return x_refs[gp][0, pl.ds(i, r0, stride=CMP_STRIDE), :]

    _cmp_core(load_rows, pe_ref, wp_ref, w2_ref, gain_ref, out_ref, r0=r0, rd=r0)


def _cmp_paged_kernel(*refs, r0, n_pages, page, has_norm):
    pages = refs[1:1 + n_pages + 1]
    i = n_pages + 2
    pe_ref, wp_ref, w2_ref = refs[i:i + 3]
    gain_ref = refs[i + 3] if has_norm else None
    out_ref, xt_refs = refs[-3], refs[-2:]
    for j, p in enumerate(pages):
        for gp in range(2):
            xt_refs[gp][j * page:(j + 1) * page, :] = p[0, gp * LANES:(gp + 1) * LANES, :].T
    rd = r0 + SUBLANES

    def load_rows(i, gp):
        return xt_refs[gp][pl.ds(i, rd, stride=CMP_STRIDE), :]

    _cmp_core(load_rows, pe_ref, wp_ref, w2_ref, gain_ref, out_ref, r0=r0, rd=rd)


def _cmp_weights(pe, w1, w2):
    hd, G = HEAD_DIM, NSA_KV_HEADS
    npair = CMP_STRIDE // 2
    w1r = w1.reshape(2, npair, 2, hd, hd)
    wp = jnp.einsum('ab,hpidn->piadbhn', jnp.eye(2, dtype=F32), w1r)
    wp = wp.reshape(npair, 4 * hd, 4 * hd).astype(BF16)
    w2g = jnp.zeros((G, LANES, G * hd), F32)
    for g in range(G):
        w2g = w2g.at[g, :hd, g * hd:(g + 1) * hd].set(w2)
    per = jnp.broadcast_to(pe.reshape(2, npair, 2, 1, hd), (2, npair, 2, 2, hd))
    per = per.transpose(1, 0, 2, 3, 4).reshape(npair, 2, 4 * hd)
    pe_rows = jnp.pad(per.astype(F32), ((0, 0), (0, SUBLANES - 2), (0, 0)))
    return pe_rows, wp, w2g.astype(BF16)


def _cmp_specs(has_norm):
    gd = NSA_KV_HEADS * HEAD_DIM
    specs = [pl.BlockSpec((CMP_STRIDE // 2, SUBLANES, 4 * HEAD_DIM), lambda *a: (0, 0, 0)),
             pl.BlockSpec((CMP_STRIDE // 2, 4 * HEAD_DIM, 4 * HEAD_DIM), lambda *a: (0, 0, 0)),
             pl.BlockSpec((NSA_KV_HEADS, LANES, gd), lambda *a: (0, 0, 0))]
    if has_norm:
        specs.append(pl.BlockSpec((1, gd), lambda *a: (0, 0)))
    return specs


def _compress_seq(x3, pe, w1, w2, gain=None):
    b, t, gd = x3.shape
    r0 = t // CMP_STRIDE
    pe_rows, wp, w2g = _cmp_weights(pe, w1, w2)
    assert gd == 2 * LANES
    args = [x3, x3, pe_rows, wp, w2g]
    if gain is not None:
        args.append(jnp.tile(gain.reshape(1, HEAD_DIM), (1, NSA_KV_HEADS)))
    return pl.pallas_call(
        functools.partial(_cmp_seq_kernel, r0=r0, has_norm=gain is not None),
        grid=(b,),
        in_specs=[pl.BlockSpec((1, t, LANES), lambda i: (i, 0, 0)), pl.BlockSpec((1, t, LANES), lambda i: (i, 0, 1))]
        + _cmp_specs(gain is not None),
        out_specs=pl.BlockSpec((1, r0, gd), lambda i: (i, 0, 0)),
        out_shape=jax.ShapeDtypeStruct((b, r0, gd), F32),
        compiler_params=_cparams("parallel"), name="compress_seq")(*args)


def _pages_gd_tok(cache):
    n_pool, page, G, hd = cache.shape
    return jnp.transpose(cache, (0, 2, 3, 1)).reshape(n_pool, G * hd, page)


def _compress_paged(cache, page_table, pe, w1, w2, gain=None):
    n_pool, page, G, hd = cache.shape
    gd = G * hd
    b, n_pages = page_table.shape
    assert page == LANES and gd == 2 * LANES and n_pages % PAGES_PER_STEP == 0
    r0 = PAGES_PER_STEP * page // CMP_STRIDE
    nq = n_pages // PAGES_PER_STEP
    cv = _pages_gd_tok(cache)
    pe_rows, wp, w2g = _cmp_weights(pe, w1, w2)

    def page_spec(j):
        return pl.BlockSpec((1, gd, page),
                            lambda i, q, pt: (pt[i * n_pages + jnp.minimum(q * PAGES_PER_STEP + j, n_pages - 1)], 0, 0))

    in_specs = [page_spec(j) for j in range(PAGES_PER_STEP + 1)] + _cmp_specs(gain is not None)
    args = [cv] * (PAGES_PER_STEP + 1) + [pe_rows, wp, w2g]
    if gain is not None:
        args.append(jnp.tile(gain.reshape(1, HEAD_DIM), (1, NSA_KV_HEADS)))
    gs = pltpu.PrefetchScalarGridSpec(
        num_scalar_prefetch=1, grid=(b, nq), in_specs=in_specs,
        out_specs=pl.BlockSpec((1, r0, gd), lambda i, q, pt: (i, q, 0)),
        scratch_shapes=[pltpu.VMEM(((PAGES_PER_STEP + 1) * page, LANES), F32)] * 2)
    return pl.pallas_call(
        functools.partial(_cmp_paged_kernel, r0=r0, n_pages=PAGES_PER_STEP, page=page, has_norm=gain is not None),
        grid_spec=gs, out_shape=jax.ShapeDtypeStruct((b, nq * r0, gd), F32),
        compiler_params=_cparams("parallel", "arbitrary"), name="compress_paged")(
            page_table.reshape(-1), *args)


def _rank_rows(imp_ref, n_cand):
    shape = imp_ref.shape
    x = imp_ref[...]
    ridx = lax.broadcasted_iota(jnp.int32, shape, 0)

    def body(sp, cnt):
        row = imp_ref[pl.ds(sp, 1), :]
        ge = jnp.where(row >= x, 1.0, 0.0)
        gt = jnp.where(row > x, 1.0, 0.0)
        return cnt + jnp.where(sp < ridx, ge, gt)

    return lax.fori_loop(0, n_cand, body, jnp.zeros(shape, F32))


def _rank_kernel(imp_ref, rank_ref, *, n_cand):
    rank_ref[...] = _rank_rows(imp_ref, n_cand)


def _rank_call(imp_t, n_cand):
    return pl.pallas_call(
        functools.partial(_rank_kernel, n_cand=n_cand),
        out_shape=jax.ShapeDtypeStruct(imp_t.shape, F32),
        compiler_params=pltpu.CompilerParams(vmem_limit_bytes=VMEM_LIMIT), name="rank")(imp_t)


def _cmpattn_kernel(q_ref, kc_ref, vc_ref, gate_ref, cov_ref, o_ref, sel_ref, imp_ref,
                    *, tq, n_cmp, n_sel, k_top):
    qi = pl.program_id(1)
    nc = kc_ref.shape[1]
    ns = cov_ref.shape[0]
    hd = HEAD_DIM
    G, R = NSA_KV_HEADS, NSA_HEADS // NSA_KV_HEADS
    scale = hd ** -0.5
    t_col = qi * tq + lax.broadcasted_iota(jnp.int32, (tq, nc), 0)
    cidx = lax.broadcasted_iota(jnp.int32, (tq, nc), 1)
    mask = (cidx * CMP_STRIDE + CMP_BLK - 1 <= t_col) & (cidx < n_cmp)
    sid = lax.broadcasted_iota(jnp.int32, (ns, tq), 0)
    tt = qi * tq + lax.broadcasted_iota(jnp.int32, (ns, tq), 1)
    bt = tt // SEL_BLK
    forced = (sid == 0) | (sid == bt) | (sid == bt - 1)
    valid = sid * SEL_BLK <= tt
    for g in range(G):
        kg = kc_ref[0, :, g * hd:(g + 1) * hd].astype(BF16)
        vg = vc_ref[0, :, g * hd:(g + 1) * hd].astype(BF16)
        psum = jnp.zeros((tq, nc), F32)
        for r in range(R):
            h = g * R + r
            qh = (q_ref[0, :, h * hd:(h + 1) * hd] * scale).astype(BF16)
            s = jnp.where(mask, _nt(qh, kg), NEG_INF)
            m = jnp.max(s, axis=-1, keepdims=True)
            m = jnp.where(m == NEG_INF, 0.0, m)
            e = jnp.exp(s - m)
            p = e / jnp.maximum(jnp.sum(e, axis=-1, keepdims=True), 1e-30)
            o = _dot(p.astype(BF16), vg)
            o_ref[0, :, h * hd:(h + 1) * hd] = o * _sigmoid(gate_ref[0, :, 3 * h:3 * h + 1])
            psum = psum + p
        p1, p2, p3 = _split3(psum)
        imp = _nt(cov_ref[...], p1) + _nt(cov_ref[...], p2) + _nt(cov_ref[...], p3)
        imp_ref[...] = jnp.where(forced, SEL_BONUS, jnp.where(valid, imp, -SEL_BONUS))
        rank = _rank_rows(imp_ref, n_sel)
        sel_ref[0, 0, g] = jnp.where(rank < k_top, 1.0, 0.0)


def _cover(n_cmp_rows, n_sel_cols):
    cstart = np.arange(n_cmp_rows)[:, None] * CMP_STRIDE
    sstart = np.arange(n_sel_cols)[None, :] * SEL_BLK
    return ((cstart < sstart + SEL_BLK) & (cstart + CMP_BLK > sstart)).astype(np.float32)


def _cmp_attn_seq(q3, kcmp, vcmp, gate3):
    b, t, _ = q3.shape
    tq = min(QBLK, t)
    nc = kcmp.shape[1]
    n_cmp = (t - CMP_BLK) // CMP_STRIDE + 1
    n_sel = -(-t // SEL_BLK)
    k_top = min(N_SEL, n_sel)
    cov_t = jnp.asarray(_cover(nc, n_sel).T, BF16)
    gd = NSA_KV_HEADS * HEAD_DIM
    return pl.pallas_call(
        functools.partial(_cmpattn_kernel, tq=tq, n_cmp=n_cmp, n_sel=n_sel, k_top=k_top),
        grid=(b, t // tq),
        in_specs=[pl.BlockSpec((1, tq, NSA_HEADS * HEAD_DIM), lambda i, j: (i, j, 0)),
                  pl.BlockSpec((1, nc, gd), lambda i, j: (i, 0, 0)),
                  pl.BlockSpec((1, nc, gd), lambda i, j: (i, 0, 0)),
                  pl.BlockSpec((1, tq, LANES), lambda i, j: (i, j, 0)),
                  pl.BlockSpec((n_sel, nc), lambda i, j: (0, 0))],
        out_specs=[pl.BlockSpec((1, tq, NSA_HEADS * HEAD_DIM), lambda i, j: (i, j, 0)),
                   pl.BlockSpec((1, 1, NSA_KV_HEADS, n_sel, tq), lambda i, j: (i, j, 0, 0, 0))],
        out_shape=[jax.ShapeDtypeStruct((b, t, NSA_HEADS * HEAD_DIM), F32),
                   jax.ShapeDtypeStruct((b, t // tq, NSA_KV_HEADS, n_sel, tq), F32)],
        scratch_shapes=[pltpu.VMEM((n_sel, tq), F32)],
        compiler_params=_cparams("parallel", "parallel"), name="cmp_attn_seq")(q3, kcmp, vcmp, gate3, cov_t)


def _cmpstep_kernel(q_ref, kc_ref, vc_ref, gate_ref, cov_ref, o_ref, imp_ref, *, n_cmp, n_sel, pos):
    nc = kc_ref.shape[1]
    ns = cov_ref.shape[1]
    hd = HEAD_DIM
    G, R = NSA_KV_HEADS, NSA_HEADS // NSA_KV_HEADS
    scale = hd ** -0.5
    cidx = lax.broadcasted_iota(jnp.int32, (R, nc), 1)
    mask = (cidx * CMP_STRIDE + CMP_BLK - 1 <= pos) & (cidx < n_cmp)
    psums = []
    for g in range(G):
        kg = kc_ref[0, :, g * hd:(g + 1) * hd].astype(BF16)
        vg = vc_ref[0, :, g * hd:(g + 1) * hd].astype(BF16)
        qg = (q_ref[0, g * R:(g + 1) * R, :] * scale).astype(BF16)
        s = jnp.where(mask, _nt(qg, kg), NEG_INF)
        m = jnp.max(s, axis=-1, keepdims=True)
        m = jnp.where(m == NEG_INF, 0.0, m)
        e = jnp.exp(s - m)
        p = e / jnp.maximum(jnp.sum(e, axis=-1, keepdims=True), 1e-30)
        o = _dot(p.astype(BF16), vg)
        o_ref[0, g * R:(g + 1) * R, :] = o * _sigmoid(gate_ref[0, g * R:(g + 1) * R, 0:1])
        psums.append(jnp.sum(p, axis=0, keepdims=True))
    p1, p2, p3 = _split3(jnp.concatenate(psums, axis=0))
    imp = _dot(p1, cov_ref[...]) + _dot(p2, cov_ref[...]) + _dot(p3, cov_ref[...])
    sid = lax.broadcasted_iota(jnp.int32, (G, ns), 1)
    bt = pos // SEL_BLK
    forced = (sid == 0) | (sid == bt) | (sid == bt - 1)
    valid = sid * SEL_BLK <= pos
    imp_ref[0] = jnp.where(forced, SEL_BONUS, jnp.where(valid, imp, -SEL_BONUS))


def _cmp_attn_step(q3, kcmp, vcmp, gate3, *, pos, n_cmp, n_sel, ns_pad):
    b = q3.shape[0]
    nc = kcmp.shape[1]
    cov = jnp.asarray(_cover(nc, ns_pad), BF16)
    gd = NSA_KV_HEADS * HEAD_DIM
    one = lambda i: (i, 0, 0)
    return pl.pallas_call(
        functools.partial(_cmpstep_kernel, n_cmp=n_cmp, n_sel=n_sel, pos=pos),
        grid=(b,),
        in_specs=[pl.BlockSpec((1, NSA_HEADS, HEAD_DIM), one), pl.BlockSpec((1, nc, gd), one),
                  pl.BlockSpec((1, nc, gd), one), pl.BlockSpec((1, NSA_HEADS, 3), one),
                  pl.BlockSpec((nc, ns_pad), lambda i: (0, 0))],
        out_specs=[pl.BlockSpec((1, NSA_HEADS, HEAD_DIM), one), pl.BlockSpec((1, NSA_KV_HEADS, ns_pad), one)],
        out_shape=[jax.ShapeDtypeStruct((b, NSA_HEADS, HEAD_DIM), F32),
                   jax.ShapeDtypeStruct((b, NSA_KV_HEADS, ns_pad), F32)],
        compiler_params=_cparams("parallel"), name="cmp_attn_step")(q3, kcmp, vcmp, gate3, cov)


def _selattn_kernel(q_ref, k_ref, v_ref, sel_ref, gate_ref, o_ref, qs_ref, m_ref, l_ref, acc_ref,
                    *, tq, tk, n_sel):
    qi = pl.program_id(1)
    hd = HEAD_DIM
    G, R = NSA_KV_HEADS, NSA_HEADS // NSA_KV_HEADS
    scale = hd ** -0.5
    n_tiles = ((qi + 1) * tq + tk - 1) // tk
    tt = qi * tq + lax.broadcasted_iota(jnp.int32, (tk, tq), 1)
    krow = lax.broadcasted_iota(jnp.int32, (tk, tq), 0)
    pad = jnp.zeros((LANES - hd, tq), F32)
    for g in range(G):
        for r in range(R):
            h = g * R + r
            qs_ref[r * tq:(r + 1) * tq, :] = (q_ref[0, :, h * hd:(h + 1) * hd] * scale).astype(BF16)
        m_ref[...] = jnp.full(m_ref.shape, NEG_INF, F32)
        l_ref[...] = jnp.zeros(l_ref.shape, F32)
        acc_ref[...] = jnp.zeros(acc_ref.shape, F32)

        def body(j, carry):
            k0 = pl.multiple_of(j * tk, tk)
            kt = k_ref[0, pl.ds(k0, tk), g * hd:(g + 1) * hd].astype(BF16)
            vt = v_ref[0, pl.ds(k0, tk), g * hd:(g + 1) * hd].astype(BF16)
            st = _nt(kt, qs_ref[...])
            b0 = j * (tk // SEL_BLK)
            chosen = jnp.concatenate(
                [jnp.broadcast_to(sel_ref[0, 0, g, pl.ds(b0 + i, 1), :], (SEL_BLK, tq))
                 for i in range(tk // SEL_BLK)], axis=0)
            neg = jnp.where((chosen > 0.5) & (k0 + krow <= tt), 0.0, NEG_INF)
            st = st + jnp.concatenate([neg] * R, axis=1)
            m_old = m_ref[...]
            m_new = jnp.maximum(m_old, jnp.max(st, axis=0, keepdims=True))
            m_safe = jnp.where(m_new == NEG_INF, 0.0, m_new)
            alpha = jnp.exp(m_old - m_safe)
            p = jnp.exp(st - m_safe)
            l_ref[...] = alpha * l_ref[...] + jnp.sum(p, axis=0, keepdims=True)
            acc_ref[...] = alpha * acc_ref[...] + _tn(vt, p.astype(BF16))
            m_ref[...] = m_new
            return carry

        lax.fori_loop(0, n_tiles, body, 0)
        accn = acc_ref[...] / jnp.maximum(l_ref[...], 1e-30)
        for r in range(R):
            h = g * R + r
            o = jnp.concatenate([accn[:, r * tq:(r + 1) * tq], pad], axis=0).T[:, :hd]
            o_ref[0, :, h * hd:(h + 1) * hd] = o * _sigmoid(gate_ref[0, :, 3 * h + 1:3 * h + 2])


def _sel_attn_seq(q3, k3, v3, sel, gate3):
    b, t, _ = q3.shape
    tq = min(QBLK, t)
    tk = min(2 * QBLK, t)
    n_sel = sel.shape[-2]
    R = NSA_HEADS // NSA_KV_HEADS
    gd = NSA_KV_HEADS * HEAD_DIM
    return pl.pallas_call(
        functools.partial(_selattn_kernel, tq=tq, tk=tk, n_sel=n_sel),
        grid=(b, t // tq),
        in_specs=[pl.BlockSpec((1, tq, NSA_HEADS * HEAD_DIM), lambda i, j: (i, j, 0)),
                  pl.BlockSpec((1, t, gd), lambda i, j: (i, 0, 0)),
                  pl.BlockSpec((1, t, gd), lambda i, j: (i, 0, 0)),
                  pl.BlockSpec((1, 1, NSA_KV_HEADS, n_sel, tq), lambda i, j: (i, j, 0, 0, 0)),
                  pl.BlockSpec((1, tq, LANES), lambda i, j: (i, j, 0))],
        out_specs=pl.BlockSpec((1, tq, NSA_HEADS * HEAD_DIM), lambda i, j: (i, j, 0)),
        out_shape=jax.ShapeDtypeStruct((b, t, NSA_HEADS * HEAD_DIM), F32),
        scratch_shapes=[pltpu.VMEM((R * tq, HEAD_DIM), BF16), pltpu.VMEM((1, R * tq), F32),
                        pltpu.VMEM((1, R * tq), F32), pltpu.VMEM((HEAD_DIM, R * tq), F32)],
        compiler_params=_cparams("parallel", "parallel"), name="sel_attn_seq")(q3, k3, v3, sel, gate3)


def _selstep_kernel(*refs, n_slots, page):
    idx_ref, half_ref, skip_ref, hasnew_ref = refs[:4]
    q_ref = refs[4]
    k_refs = refs[5:5 + n_slots]
    v_refs = refs[5 + n_slots:5 + 2 * n_slots]
    nk_ref, nv_ref, gate_ref, o_ref = refs[5 + 2 * n_slots:]
    b = pl.program_id(0)
    g = pl.program_id(1)
    base = (b * NSA_KV_HEADS + g) * n_slots
    hd = HEAD_DIM
    R = NSA_HEADS // NSA_KV_HEADS
    scale = hd ** -0.5
    qg = (q_ref[0, 0] * scale).astype(BF16)
    lane_half = lax.broadcasted_iota(jnp.int32, (R, page), 1) // SEL_BLK
    nk = nk_ref[0, 0]
    nv = nv_ref[0, 0]
    s_new = _nt(qg, jnp.broadcast_to(nk, (SUBLANES, hd)).astype(BF16))[:, 0:1]
    s_new = jnp.where(hasnew_ref[b * NSA_KV_HEADS + g] == 1, s_new, NEG_INF)
    m = s_new
    scores = []
    for j in range(n_slots):
        s = _dot(qg, k_refs[j][0, 0].astype(BF16))
        s = jnp.where(lane_half == half_ref[base + j], s, NEG_INF)
        s = jnp.where(skip_ref[base + j] == 0, s, NEG_INF)
        m = jnp.maximum(m, jnp.max(s, axis=-1, keepdims=True))
        scores.append(s)
    m = jnp.where(m == NEG_INF, 0.0, m)
    e_new = jnp.exp(s_new - m)
    den = e_new
    es = []
    for s in scores:
        e = jnp.exp(s - m)
        den = den + jnp.sum(e, axis=-1, keepdims=True)
        es.append(e)
    den = jnp.maximum(den, 1e-30)
    o = (e_new / den) * nv
    for e, v_ref in zip(es, v_refs):
        o = o + _nt((e / den).astype(BF16), v_ref[0, 0].astype(BF16))
    o_ref[0, 0] = o * _sigmoid(gate_ref[0, 0, :, 1:2])


def _sel_attn_step(q4, cache_k, cache_v, page_idx, half, skip, has_new, new_k, new_v, gate4):
    b = q4.shape[0]
    n_pool, page, G, hd = cache_k.shape
    R = NSA_HEADS // NSA_KV_HEADS
    ck = _pages_gd_tok(cache_k).reshape(n_pool, G, hd, page)
    cv = _pages_gd_tok(cache_v).reshape(n_pool, G, hd, page)
    n_slots = N_SEL

    def slot_spec(j):
        return pl.BlockSpec((1, 1, hd, page),
                            lambda i, g, idx, hf, sk, hn: (idx[(i * G + g) * n_slots + j], g, 0, 0))

    grp = lambda i, g, idx, hf, sk, hn: (i, g, 0, 0)
    in_specs = ([pl.BlockSpec((1, 1, R, hd), grp)] + [slot_spec(j) for j in range(n_slots)] * 2
                + [pl.BlockSpec((1, 1, 1, hd), grp)] * 2 + [pl.BlockSpec((1, 1, R, 3), grp)])
    gs = pltpu.PrefetchScalarGridSpec(
        num_scalar_prefetch=4, grid=(b, G), in_specs=in_specs,
        out_specs=pl.BlockSpec((1, 1, R, hd), grp))
    return pl.pallas_call(
        functools.partial(_selstep_kernel, n_slots=n_slots, page=page), grid_spec=gs,
        out_shape=jax.ShapeDtypeStruct((b, G, R, hd), F32),
        compiler_params=_cparams("arbitrary", "arbitrary"), name="sel_attn_step")(
            page_idx, half, skip, has_new, q4, *([ck] * n_slots), *([cv] * n_slots), new_k, new_v, gate4)


def _mem_kernel(q_ref, k_ref, v_ref, g_ref, o_ref):
    hd = MEM_HEAD_DIM
    for h in range(MEM_HEADS):
        sl = slice(h * hd, (h + 1) * hd)
        qh = q_ref[0, :, sl]
        qn = qh * lax.rsqrt(jnp.mean(qh * qh, axis=-1, keepdims=True) + EPS) * g_ref[...]
        s = _nt(qn.astype(BF16), k_ref[0, :, sl].astype(BF16)) * (hd ** -0.5)
        m = jnp.max(s, axis=-1, keepdims=True)
        e = jnp.exp(s - m)
        p = e / jnp.sum(e, axis=-1, keepdims=True)
        o_ref[0, :, sl] = _dot(p.astype(BF16), v_ref[0, :, sl].astype(BF16))


def _mem_attn(q3, k3, v3, gq):
    b, t, wd = q3.shape
    mt = k3.shape[1]
    tm = _row_tile(t, 512)
    return pl.pallas_call(
        _mem_kernel, grid=(b, t // tm),
        in_specs=[pl.BlockSpec((1, tm, wd), lambda i, j: (i, j, 0)),
                  pl.BlockSpec((1, mt, wd), lambda i, j: (i, 0, 0)),
                  pl.BlockSpec((1, mt, wd), lambda i, j: (i, 0, 0)),
                  pl.BlockSpec((1, MEM_HEAD_DIM), lambda i, j: (0, 0))],
        out_specs=pl.BlockSpec((1, tm, wd), lambda i, j: (i, j, 0)),
        out_shape=jax.ShapeDtypeStruct((b, t, wd), F32),
        compiler_params=_cparams("parallel", "parallel"), name="mem_attn")(
            q3, k3, v3, gq.reshape(1, MEM_HEAD_DIM))


FFN_COLS = 256


def _ffn_kernel(*refs, ff, state_mode):
    if state_mode:
        u_ref, p2_ref, p1_ref, cw_ref, cb_ref, wo_ref, x_ref, o_ref, acc_ref = refs
    else:
        u_ref, halo_ref, cw_ref, cb_ref, wo_ref, x_ref, o_ref, acc_ref = refs
        first = pl.program_id(1) == 0

    def conv(off):
        sl = slice(off, off + FFN_COLS)
        if state_mode:
            cur, s1, s2 = u_ref[:, sl], p1_ref[:, sl], p2_ref[:, sl]
        else:
            cur = u_ref[0, :, sl]
            halo = jnp.where(first, 0.0, halo_ref[0, :, sl])
            s1 = _shift_rows(cur, halo, 1)
            s2 = _shift_rows(cur, halo, 2)
        return s2 * cw_ref[0:1, sl] + s1 * cw_ref[1:2, sl] + cur * cw_ref[2:3, sl] + cb_ref[:, sl]

    acc_ref[...] = x_ref[...] if state_mode else x_ref[0]
    for c in range(ff // FFN_COLS):
        a = conv(c * FFN_COLS)
        b = conv(ff + c * FFN_COLS)
        act = (_silu(a) * b).astype(BF16)
        acc_ref[...] += _dot(act, wo_ref[c * FFN_COLS:(c + 1) * FFN_COLS, :])
    if state_mode:
        o_ref[...] = acc_ref[...]
    else:
        o_ref[0] = acc_ref[...]


def _ffn_tail_seq(u3, cw8, cb, wo, x3):
    b, t, f2 = u3.shape
    ff = f2 // 2
    d = x3.shape[-1]
    assert ff % FFN_COLS == 0
    tm = _row_tile(t, 256)
    hb = tm // SUBLANES
    return pl.pallas_call(
        functools.partial(_ffn_kernel, ff=ff, state_mode=False),
        grid=(b, t // tm),
        in_specs=[pl.BlockSpec((1, tm, f2), lambda i, j: (i, j, 0)),
                  pl.BlockSpec((1, SUBLANES, f2), lambda i, j: (i, jnp.maximum(j * hb - 1, 0), 0)),
                  pl.BlockSpec((SUBLANES, f2), lambda i, j: (0, 0)),
                  pl.BlockSpec((1, f2), lambda i, j: (0, 0)),
                  pl.BlockSpec((ff, d), lambda i, j: (0, 0)),
                  pl.BlockSpec((1, tm, d), lambda i, j: (i, j, 0))],
        out_specs=pl.BlockSpec((1, tm, d), lambda i, j: (i, j, 0)),
        out_shape=jax.ShapeDtypeStruct((b, t, d), F32),
        scratch_shapes=[pltpu.VMEM((tm, d), F32)],
        compiler_params=_cparams("parallel", "arbitrary"), name="ffn_tail_seq")(u3, u3, cw8, cb, wo, x3)


def _ffn_tail_step(u, prev2, prev1, cw8, cb, wo, x):
    m, f2 = u.shape
    ff = f2 // 2
    d = x.shape[-1]
    tm = _row_tile(m, 128)
    row = pl.BlockSpec((tm, f2), lambda i: (i, 0))
    return pl.pallas_call(
        functools.partial(_ffn_kernel, ff=ff, state_mode=True),
        grid=(m // tm,),
        in_specs=[row, row, row,
                  pl.BlockSpec((SUBLANES, f2), lambda i: (0, 0)), pl.BlockSpec((1, f2), lambda i: (0, 0)),
                  pl.BlockSpec((ff, d), lambda i: (0, 0)), pl.BlockSpec((tm, d), lambda i: (i, 0))],
        out_specs=pl.BlockSpec((tm, d), lambda i: (i, 0)),
        out_shape=jax.ShapeDtypeStruct((m, d), F32),
        scratch_shapes=[pltpu.VMEM((tm, d), F32)],
        compiler_params=_cparams("parallel"), name="ffn_tail_step")(u, prev2, prev1, cw8, cb, wo, x)


def _pad_rows(a, rows):
    return jnp.pad(a.astype(F32), ((0, rows - a.shape[0]), (0, 0)))


def _pad_cols(w, cols):
    return jnp.pad(w, ((0, 0), (0, cols - w.shape[1])))


def _mem_block(x, bsz, t, mem_k3, mem_v3, P, layer):
    qm = _proj([x], P['mem_w_q'][layer], gain=P['mem_norm_x'][layer])
    om = _mem_attn(qm.reshape(bsz, t, -1), mem_k3, mem_v3, P['mem_q_norm'][layer])
    return _proj([om.reshape(bsz * t, -1)], P['mem_w_o'][layer], res=x)


def _ffn_in(x, P, layer):
    f2 = P['ffn_w_in'][layer].shape[1]
    tn = f2 // 4 if (f2 // 4) % LANES == 0 else f2
    return _proj([x], P['ffn_w_in'][layer], gain=P['ffn_norm'][layer], tn=tn)


def _dn_params(P):
    pa = jnp.zeros((1, LANES), F32).at[0, DN_HEADS:2 * DN_HEADS].set(P['l0_dn_a_log'].astype(F32))
    pd = jnp.zeros((1, LANES), F32).at[0, DN_HEADS:2 * DN_HEADS].set(P['l0_dn_dt_bias'].astype(F32))
    return _pad_rows(P['l0_dn_conv_w'], SUBLANES), pa, pd, P['l0_dn_norm'].astype(F32).reshape(1, DN_DK)


L0_SEG = (1536, 512, LANES, 512, 128, 128)
L1_SEG = (1024, 256, 256, 256, 256, 256, 256, LANES)


def _prompt_group(x_prompt, mem_prompt, P):
    bsz, t, d = x_prompt.shape
    m = bsz * t
    x = x_prompt.reshape(m, d)
    tabs = _rope_tables(jnp.arange(t, dtype=jnp.int32))

    mt = mem_prompt.shape[1]
    mem_rows = mem_prompt.reshape(bsz * mt, d)
    mem_k, mem_v = [], []
    for layer in range(2):
        kraw, v = _proj([mem_rows], P['mem_w_kv'][layer], gain=P['mem_norm_kv'][layer],
                        seg=(MEM_HEADS * MEM_HEAD_DIM,) * 2)
        mem_k.append(_headnorm(kraw, P['mem_k_norm'][layer], MEM_HEAD_DIM).reshape(bsz, mt, -1))
        mem_v.append(v.reshape(bsz, mt, -1))

    qkv, z, ba, q_b, k_b, v_b = _proj([x], P['l0_w_in'], gain=P['mix_norm'][0], seg=L0_SEG)
    cw8, pa, pd, gn = _dn_params(P)
    qkv3 = qkv.reshape(bsz, t, -1)
    w, u, qg, kd, qk, gc = _dn_prepare(qkv3, ba.reshape(bsz, t, LANES), cw8, pa, pd)
    o_dn, s_fin = _dn_recur(w, u, qg, kd, qk, gc, z.reshape(bsz, t, -1), gn)
    qs = _headnorm(q_b, P['l0_swa_q_norm'], HEAD_DIM, tabs=tabs, seq_len=t, want_norm=False)
    ks = _headnorm(k_b, P['l0_swa_k_norm'], HEAD_DIM, tabs=tabs, seq_len=t, want_norm=False)
    ks3 = ks.reshape(bsz, t, -1)
    vs3 = v_b.reshape(bsz, t, -1)
    o_sw = _band_attn(qs.reshape(bsz, t, -1), ks3, vs3, H=SWA_HEADS, G=SWA_KV_HEADS, window=SWA_WINDOW,
                      sinks=P['l0_swa_sinks'])
    mix = jnp.concatenate([o_dn, o_sw], axis=-1).reshape(m, -1)
    x = _proj([mix], P['l0_w_out'], res=x)
    x = _mem_block(x, bsz, t, mem_k[0], mem_v[0], P, 0)
    u0 = _ffn_in(x, P, 0)
    u03 = u0.reshape(bsz, t, -1)
    x = _ffn_tail_seq(u03, P['ffn_cw8'][0], P['ffn_cb'][0], P['ffn_w_out'][0], x.reshape(bsz, t, d)).reshape(m, d)
    nb = min(SWA_WINDOW, t)
    ab_state = (s_fin, qkv3[:, t - 3:], ks3[:, t - nb:].reshape(bsz, nb, SWA_KV_HEADS, HEAD_DIM),
                vs3[:, t - nb:].reshape(bsz, nb, SWA_KV_HEADS, HEAD_DIM))

    q, kc, vc, ksr, vsel, kwr, vw, gr = _proj([x], P['l1_w_in'], gain=P['mix_norm'][1], seg=L1_SEG)
    qn, qr = _headnorm(q, P['l1_q_norm'], HEAD_DIM, tabs=tabs, seq_len=t)
    ksel = _headnorm(ksr, P['l1_k_norm_sel'], HEAD_DIM, tabs=tabs, seq_len=t, want_norm=False)
    kw = _headnorm(kwr, P['l1_k_norm_win'], HEAD_DIM, tabs=tabs, seq_len=t, want_norm=False)
    gd = NSA_KV_HEADS * HEAD_DIM
    r3 = lambda a: a.reshape(bsz, t, -1)
    kcmp = _compress_seq(r3(kc), P['l1_cmp_pe_k'], P['l1_cmp_w1_k'], P['l1_cmp_w2_k'], P['l1_k_norm_cmp'])
    vcmp = _compress_seq(r3(vc), P['l1_cmp_pe_v'], P['l1_cmp_w1_v'], P['l1_cmp_w2_v'])
    gr3 = r3(gr)
    o_c, sel = _cmp_attn_seq(r3(qn), kcmp, vcmp, gr3)
    o_s = _sel_attn_seq(r3(qr), r3(ksel), r3(vsel), sel, gr3)
    o_w = _band_attn(r3(qr), r3(kw), r3(vw), H=NSA_HEADS, G=NSA_KV_HEADS, window=NSA_WINDOW,
                     gate3=gr3, gate_col=2)
    x = _proj([o_c.reshape(m, -1), o_s.reshape(m, -1), o_w.reshape(m, -1)], P['l1_w_out'], res=x)
    x = _mem_block(x, bsz, t, mem_k[1], mem_v[1], P, 1)
    u1 = _ffn_in(x, P, 1)
    u13 = u1.reshape(bsz, t, -1)
    x = _ffn_tail_seq(u13, P['ffn_cw8'][1], P['ffn_cb'][1], P['ffn_w_out'][1], x.reshape(bsz, t, d)).reshape(m, d)
    nw = min(NSA_WINDOW, t)
    r4 = lambda a: a.reshape(bsz, t, NSA_KV_HEADS, HEAD_DIM)
    nsa_state = (r4(kc), r4(vc), r4(ksel), r4(vsel), r4(kw)[:, t - nw:], r4(vw)[:, t - nw:])
    mem_kr = jnp.stack([k.reshape(bsz, mt, MEM_HEADS, MEM_HEAD_DIM) for k in mem_k])
    mem_vr = jnp.stack([v.reshape(bsz, mt, MEM_HEADS, MEM_HEAD_DIM) for v in mem_v])
    ffn_state = jnp.stack([u03[:, t - 2:], u13[:, t - 2:]])
    return x.reshape(bsz, t, d), ab_state, nsa_state, mem_kr, mem_vr, ffn_state


def _sample_group(x_sample, state_dn, state_dn_conv, cache_swa_k, cache_swa_v, cache_cmp_k, cache_cmp_v,
                  cache_sel_k, cache_sel_v, cache_win_k, cache_win_v, cache_mem_k, cache_mem_v,
                  state_ffn_conv, page_table, P):
    bsz, t, d = x_sample.shape
    assert t == 1
    page = cache_cmp_k.shape[1]
    n_pages = page_table.shape[1]
    past = n_pages * page
    assert cache_swa_k.shape[1] == SWA_WINDOW and cache_win_k.shape[1] == NSA_WINDOW
    x = x_sample.reshape(bsz, d)
    tabs = _rope_tables(jnp.full((1,), past, jnp.int32))
    mt = cache_mem_k.shape[2]

    qkv, z, ba, q_b, k_b, v_b = _proj([x], P['l0_w_in'], gain=P['mix_norm'][0], seg=L0_SEG)
    cw8, pa, pd, gn = _dn_params(P)
    o_dn, s_new, conv_new = _dn_step(qkv, state_dn_conv, ba, z, state_dn, cw8, pa, pd, gn)
    qs = _headnorm(q_b, P['l0_swa_q_norm'], HEAD_DIM, tabs=tabs, want_norm=False)
    ks = _headnorm(k_b, P['l0_swa_k_norm'], HEAD_DIM, tabs=tabs, want_norm=False)
    swa_k, swa_v, o_sw = _step_window_attn(
        qs.reshape(bsz, SWA_HEADS, HEAD_DIM), cache_swa_k.reshape(bsz, SWA_WINDOW, -1),
        cache_swa_v.reshape(bsz, SWA_WINDOW, -1), ks, v_b, H=SWA_HEADS, G=SWA_KV_HEADS,
        sinks=P['l0_swa_sinks'])
    mix = jnp.concatenate([o_dn.reshape(bsz, -1), o_sw.reshape(bsz, -1)], axis=-1)
    x = _proj([mix], P['l0_w_out'], res=x)
    x = _mem_block(x, bsz, 1, cache_mem_k[0].reshape(bsz, mt, -1), cache_mem_v[0].reshape(bsz, mt, -1), P, 0)
    u0 = _ffn_in(x, P, 0)
    x = _ffn_tail_step(u0, state_ffn_conv[0, :, 0], state_ffn_conv[0, :, 1], P['ffn_cw8'][0], P['ffn_cb'][0],
                       P['ffn_w_out'][0], x)
    ab_state = (s_new, conv_new, swa_k.reshape(bsz, SWA_WINDOW, SWA_KV_HEADS, HEAD_DIM),
                swa_v.reshape(bsz, SWA_WINDOW, SWA_KV_HEADS, HEAD_DIM))

    q, kc, vc, ksr, vsel, kwr, vw, gr = _proj([x], P['l1_w_in'], gain=P['mix_norm'][1], seg=L1_SEG)
    qn, qr = _headnorm(q, P['l1_q_norm'], HEAD_DIM, tabs=tabs)
    ksel = _headnorm(ksr, P['l1_k_norm_sel'], HEAD_DIM, tabs=tabs, want_norm=False)
    kw = _headnorm(kwr, P['l1_k_norm_win'], HEAD_DIM, tabs=tabs, want_norm=False)
    G, R = NSA_KV_HEADS, NSA_HEADS // NSA_KV_HEADS
    total = past + 1
    n_cmp = (total - CMP_BLK) // CMP_STRIDE + 1
    n_sel = -(-total // SEL_BLK)
    ns_pad = -(-n_sel // LANES) * LANES
    kcmp = _compress_paged(cache_cmp_k, page_table, P['l1_cmp_pe_k'], P['l1_cmp_w1_k'], P['l1_cmp_w2_k'],
                           P['l1_k_norm_cmp'])
    vcmp = _compress_paged(cache_cmp_v, page_table, P['l1_cmp_pe_v'], P['l1_cmp_w1_v'], P['l1_cmp_w2_v'])
    gate_h = gr[:, :3 * NSA_HEADS].reshape(bsz, NSA_HEADS, 3)
    o_c, imp = _cmp_attn_step(qn.reshape(bsz, NSA_HEADS, HEAD_DIM), kcmp, vcmp, gate_h, pos=past, n_cmp=n_cmp,
                              n_sel=n_sel, ns_pad=ns_pad)
    imp_t = imp.reshape(bsz * G, ns_pad).T
    rank = _rank_call(imp_t, n_sel)
    rank = jnp.where(jnp.arange(ns_pad)[:, None] < n_sel, rank, 1e9)
    blocks = jnp.argsort(rank, axis=0)[:N_SEL].T.astype(jnp.int32)
    blocks = blocks.reshape(bsz, G, N_SEL)
    per_page = page // SEL_BLK
    is_new = blocks >= n_pages * per_page
    blk_c = jnp.minimum(blocks, n_pages * per_page - 1)
    page_idx = jnp.take_along_axis(page_table[:, None, :], blk_c // per_page, axis=2)
    o_s = _sel_attn_step(
        qr.reshape(bsz, G, R, HEAD_DIM), cache_sel_k, cache_sel_v, page_idx.reshape(-1).astype(jnp.int32),
        (blk_c % per_page).reshape(-1).astype(jnp.int32), is_new.reshape(-1).astype(jnp.int32),
        jnp.any(is_new, axis=-1).reshape(-1).astype(jnp.int32), ksel.reshape(bsz, G, 1, HEAD_DIM),
        vsel.reshape(bsz, G, 1, HEAD_DIM), gate_h.reshape(bsz, G, R, 3))
    win_k, win_v, o_w = _step_window_attn(
        qr.reshape(bsz, NSA_HEADS, HEAD_DIM), cache_win_k.reshape(bsz, NSA_WINDOW, -1),
        cache_win_v.reshape(bsz, NSA_WINDOW, -1), kw, vw, H=NSA_HEADS, G=NSA_KV_HEADS,
        gate=gate_h, gate_col=2)
    x = _proj([o_c.reshape(bsz, -1), o_s.reshape(bsz, -1), o_w.reshape(bsz, -1)], P['l1_w_out'], res=x)
    x = _mem_block(x, bsz, 1, cache_mem_k[1].reshape(bsz, mt, -1), cache_mem_v[1].reshape(bsz, mt, -1), P, 1)
    u1 = _ffn_in(x, P, 1)
    x = _ffn_tail_step(u1, state_ffn_conv[1, :, 0], state_ffn_conv[1, :, 1], P['ffn_cw8'][1], P['ffn_cb'][1],
                       P['ffn_w_out'][1], x)
    r4 = lambda a: a.reshape(bsz, 1, G, HEAD_DIM)
    nsa_state = (r4(kc), r4(vc), r4(ksel), r4(vsel), win_k.reshape(bsz, NSA_WINDOW, G, HEAD_DIM),
                 win_v.reshape(bsz, NSA_WINDOW, G, HEAD_DIM))
    ffn_state = jnp.stack([jnp.stack([state_ffn_conv[0, :, 1], u0], axis=1),
                           jnp.stack([state_ffn_conv[1, :, 1], u1], axis=1)])
    return x.reshape(bsz, 1, d), ab_state, nsa_state, ffn_state


def kernel(x_prompt, x_sample, state_dn, state_dn_conv, cache_swa_k, cache_swa_v, cache_cmp_k, cache_cmp_v, cache_sel_k, cache_sel_v, cache_win_k, cache_win_v, cache_mem_k, cache_mem_v, state_ffn_conv, page_table, mem_prompt, mix_norm, l0_w_in, l0_dn_conv_w, l0_dn_a_log, l0_dn_dt_bias, l0_dn_norm, l0_swa_q_norm, l0_swa_k_norm, l0_swa_sinks, l0_w_out, l1_w_in, l1_q_norm, l1_k_norm_cmp, l1_k_norm_sel, l1_k_norm_win, l1_cmp_pe_k, l1_cmp_w1_k, l1_cmp_w2_k, l1_cmp_pe_v, l1_cmp_w1_v, l1_cmp_w2_v, l1_w_out, mem_norm_x, mem_norm_kv, mem_w_q, mem_w_k, mem_w_v, mem_q_norm, mem_k_norm, mem_w_o, ffn_norm, ffn_w_in, ffn_conv_w, ffn_conv_b, ffn_w_out):
    n_b = 2 * DN_HEADS
    c0 = 1536 + 512
    w0 = jnp.concatenate([l0_w_in[:, :c0], _pad_cols(l0_w_in[:, c0:c0 + n_b], LANES), l0_w_in[:, c0 + n_b:]], axis=1)
    c1 = (NSA_HEADS + 6 * NSA_KV_HEADS) * HEAD_DIM
    w1 = jnp.concatenate([l1_w_in[:, :c1], _pad_cols(l1_w_in[:, c1:], LANES)], axis=1)
    P = dict(
        mix_norm=mix_norm, l0_w_in=w0.astype(BF16), l0_dn_conv_w=l0_dn_conv_w, l0_dn_a_log=l0_dn_a_log,
        l0_dn_dt_bias=l0_dn_dt_bias, l0_dn_norm=l0_dn_norm, l0_swa_q_norm=l0_swa_q_norm,
        l0_swa_k_norm=l0_swa_k_norm, l0_swa_sinks=l0_swa_sinks, l0_w_out=l0_w_out.astype(BF16),
        l1_w_in=w1.astype(BF16), l1_q_norm=l1_q_norm, l1_k_norm_cmp=l1_k_norm_cmp, l1_k_norm_sel=l1_k_norm_sel,
        l1_k_norm_win=l1_k_norm_win, l1_cmp_pe_k=l1_cmp_pe_k, l1_cmp_w1_k=l1_cmp_w1_k, l1_cmp_w2_k=l1_cmp_w2_k,
        l1_cmp_pe_v=l1_cmp_pe_v, l1_cmp_w1_v=l1_cmp_w1_v, l1_cmp_w2_v=l1_cmp_w2_v,
        l1_w_out=l1_w_out.astype(BF16), mem_norm_x=mem_norm_x, mem_norm_kv=mem_norm_kv,
        mem_w_q=mem_w_q.astype(BF16), mem_w_kv=jnp.concatenate([mem_w_k, mem_w_v], axis=-1).astype(BF16),
        mem_q_norm=mem_q_norm, mem_k_norm=mem_k_norm, mem_w_o=mem_w_o.astype(BF16), ffn_norm=ffn_norm,
        ffn_w_in=ffn_w_in.astype(BF16),
        ffn_cw8=jnp.pad(ffn_conv_w.astype(F32), ((0, 0), (0, SUBLANES - ffn_conv_w.shape[1]), (0, 0))),
        ffn_cb=ffn_conv_b.astype(F32)[:, None, :], ffn_w_out=ffn_w_out.astype(BF16))

    y_p, ab_p, nsa_p, p_mem_k, p_mem_v, p_ffn = _prompt_group(x_prompt, mem_prompt, P)
    y_s, ab_s, nsa_s, s_ffn = _sample_group(
        x_sample, state_dn, state_dn_conv, cache_swa_k, cache_swa_v, cache_cmp_k, cache_cmp_v, cache_sel_k,
        cache_sel_v, cache_win_k, cache_win_v, cache_mem_k, cache_mem_v, state_ffn_conv, page_table, P)
    return (y_p, y_s, *ab_p, *nsa_p, p_mem_k, p_mem_v, p_ffn, *ab_s, *nsa_s, s_ffn)
```

```python
import functools

import numpy as np
import jax
import jax.numpy as jnp
from jax import lax
from jax.experimental import pallas as pl
from jax.experimental.pallas import tpu as pltpu

F32 = jnp.float32
BF16 = jnp.bfloat16
HI = lax.Precision.HIGHEST
NEG_INF = float("-inf")

EPS = 1e-6
HEAD_DIM = 64
ROT_DIM = HEAD_DIM // 4
ROPE_THETA = 500000.0
QBLK = 128
DN_HEADS = 4
DN_DK = 128
DN_CHUNK = 64
SWA_HEADS = 8
SWA_KV_HEADS = 2
SWA_WINDOW = 128
NSA_HEADS = 16
NSA_KV_HEADS = 4
CMP_BLK = 32
CMP_STRIDE = 16
SEL_BLK = 64
N_SEL = 16
NSA_WINDOW = 512
SEL_BONUS = 1e4
MEM_HEADS = 4
MEM_HEAD_DIM = 128

LANES = 128
SUBLANES = 8
VMEM_LIMIT = 56 << 20
PAGES_PER_STEP = 32


def _cparams(*sem):
    return pltpu.CompilerParams(dimension_semantics=sem, vmem_limit_bytes=VMEM_LIMIT)


def _nt(a, b, precision=None):
    return lax.dot_general(a, b, (((1,), (1,)), ((), ())), precision=precision,
                           preferred_element_type=F32)


def _tn(a, b, precision=None):
    return lax.dot_general(a, b, (((0,), (0,)), ((), ())), precision=precision,
                           preferred_element_type=F32)


def _dot(a, b, precision=None):
    return jnp.dot(a, b, precision=precision, preferred_element_type=F32)


def _hi_lo(a):
    hi = a.astype(BF16)
    return hi, (a - hi.astype(F32)).astype(BF16)


def _dot3(a, b):
    ah, al = _hi_lo(a)
    bh, bl = _hi_lo(b)
    return _dot(ah, bh) + _dot(ah, bl) + _dot(al, bh)


def _nt3(a, b):
    ah, al = _hi_lo(a)
    bh, bl = _hi_lo(b)
    return _nt(ah, bh) + _nt(ah, bl) + _nt(al, bh)


def _tn3(a, b):
    ah, al = _hi_lo(a)
    bh, bl = _hi_lo(b)
    return _tn(ah, bh) + _tn(ah, bl) + _tn(al, bh)


def _silu(x):
    return x / (1.0 + jnp.exp(-x))


def _sigmoid(x):
    return 1.0 / (1.0 + jnp.exp(-x))


def _softplus(x):
    return jnp.maximum(x, 0.0) + jnp.log(1.0 + jnp.exp(-jnp.abs(x)))


def _split3(x):
    a = x.astype(BF16)
    r = x - a.astype(F32)
    b = r.astype(BF16)
    c = (r - b.astype(F32)).astype(BF16)
    return a, b, c


def _row_tile(m, pref):
    for t in (1024, 512, 256, 128, 64, 32, 16, 8):
        if t <= pref and m % t == 0:
            return t
    return m


def _shift_rows(cur, halo, k):
    r = pltpu.roll(cur, k, 0)
    h = pltpu.roll(halo, k, 0)
    row = lax.broadcasted_iota(jnp.int32, h.shape, 0)
    top = jnp.where(row < k, h, r[0:SUBLANES])
    if cur.shape[0] == SUBLANES:
        return top
    return jnp.concatenate([top, r[SUBLANES:]], axis=0)


def _proj_kernel(*refs, n_add, has_gain, has_res, seg):
    it = iter(refs)
    adds = [next(it) for _ in range(n_add)]
    gain = next(it) if has_gain else None
    w = next(it)
    res = next(it) if has_res else None
    outs = [next(it) for _ in seg]
    xn = next(it)

    @pl.when(pl.program_id(1) == 0)
    def _():
        x = adds[0][...]
        for a in adds[1:]:
            x = x + a[...]
        if has_gain:
            ms = jnp.mean(x * x, axis=-1, keepdims=True)
            x = x * lax.rsqrt(ms + EPS) * gain[...]
        xn[...] = x.astype(BF16)

    acc = _dot(xn[...], w[...])
    if has_res:
        acc = acc + res[...]
    if len(seg) == 1:
        outs[0][...] = acc
    else:
        off = 0
        for o, s in zip(outs, seg):
            o[...] = acc[:, off:off + s]
            off += s


def _proj(adds, w, *, gain=None, res=None, seg=None, tn=None, tm_pref=512):
    m, k = adds[0].shape
    n = w.shape[1]
    tm = _row_tile(m, tm_pref)
    tn = n if tn is None else tn
    seg = (n,) if seg is None else tuple(seg)
    assert n % tn == 0 and (len(seg) == 1 or tn == n) and sum(seg) == n
    in_specs = [pl.BlockSpec((tm, k), lambda i, j: (i, 0)) for _ in adds]
    args = list(adds)
    if gain is not None:
        in_specs.append(pl.BlockSpec((1, k), lambda i, j: (0, 0)))
        args.append(gain.reshape(1, k))
    in_specs.append(pl.BlockSpec((k, tn), lambda i, j: (0, j)))
    args.append(w)
    if res is not None:
        in_specs.append(pl.BlockSpec((tm, tn), lambda i, j: (i, j)))
        args.append(res)
    if len(seg) == 1:
        out_shape = [jax.ShapeDtypeStruct((m, n), F32)]
        out_specs = [pl.BlockSpec((tm, tn), lambda i, j: (i, j))]
    else:
        out_shape = [jax.ShapeDtypeStruct((m, s), F32) for s in seg]
        out_specs = [pl.BlockSpec((tm, s), lambda i, j: (i, 0)) for s in seg]
    outs = pl.pallas_call(
        functools.partial(_proj_kernel, n_add=len(adds), has_gain=gain is not None,
                          has_res=res is not None, seg=seg),
        grid=(m // tm, n // tn), in_specs=in_specs, out_specs=out_specs, out_shape=out_shape,
        scratch_shapes=[pltpu.VMEM((tm, k), BF16)],
        compiler_params=_cparams("parallel", "arbitrary"), name="proj")(*args)
    return outs[0] if len(seg) == 1 else outs


def _hn_kernel(*refs, hd, want_norm, want_rope):
    x_ref, g_ref = refs[0], refs[1]
    i = 2
    if want_rope:
        c_ref, sa_ref, sb_ref = refs[i:i + 3]
        i += 3
    outs = refs[i:]
    width = x_ref.shape[1]
    r = lax.broadcasted_iota(jnp.int32, (LANES, LANES), 0) // hd
    c = lax.broadcasted_iota(jnp.int32, (LANES, LANES), 1) // hd
    bd = jnp.where(r == c, 1.0, 0.0).astype(BF16)
    for cb in range(width // LANES):
        sl = slice(cb * LANES, (cb + 1) * LANES)
        xc = x_ref[:, sl]
        x2 = xc * xc
        hi = x2.astype(BF16)
        lo = (x2 - hi.astype(F32)).astype(BF16)
        ss = _dot(hi, bd) + _dot(lo, bd)
        xc = xc * lax.rsqrt(ss * (1.0 / hd) + EPS) * g_ref[:, sl]
        k = 0
        if want_norm:
            outs[k][:, sl] = xc
            k += 1
        if want_rope:
            outs[k][:, sl] = (xc * c_ref[...] + pltpu.roll(xc, LANES - ROT_DIM // 2, 1) * sa_ref[...]
                              + pltpu.roll(xc, ROT_DIM // 2, 1) * sb_ref[...])


def _rope_tables(pos):
    half = ROT_DIM // 2
    inv = ROPE_THETA ** (-jnp.arange(half, dtype=F32) / half)
    ang = pos.astype(F32)[:, None] * inv[None, :]
    jj = np.arange(LANES) % HEAD_DIM
    cos = jnp.cos(ang)[:, jj % half]
    sin = jnp.sin(ang)[:, jj % half]
    c = jnp.where(jj[None, :] < ROT_DIM, cos, 1.0)
    sa = jnp.where(jj[None, :] < half, -sin, 0.0)
    sb = jnp.where((jj[None, :] >= half) & (jj[None, :] < ROT_DIM), sin, 0.0)
    return c, sa, sb


def _headnorm(x, gain, hd, *, tabs=None, seq_len=None, want_norm=True):
    m, width = x.shape
    want_rope = tabs is not None
    tm = _row_tile(seq_len if (want_rope and tabs[0].shape[0] > 1) else m, 512)
    g = jnp.tile(gain.reshape(1, hd), (1, width // hd))
    in_specs = [pl.BlockSpec((tm, width), lambda i: (i, 0)), pl.BlockSpec((1, width), lambda i: (0, 0))]
    args = [x, g]
    if want_rope:
        if tabs[0].shape[0] > 1:
            nt = seq_len // tm
            tspec = pl.BlockSpec((tm, LANES), lambda i: (i % nt, 0))
        else:
            tspec = pl.BlockSpec((1, LANES), lambda i: (0, 0))
        in_specs += [tspec] * 3
        args += list(tabs)
    n_out = int(want_norm) + int(want_rope)
    outs = pl.pallas_call(
        functools.partial(_hn_kernel, hd=hd, want_norm=want_norm, want_rope=want_rope),
        grid=(m // tm,), in_specs=in_specs,
        out_specs=[pl.BlockSpec((tm, width), lambda i: (i, 0))] * n_out,
        out_shape=[jax.ShapeDtypeStruct((m, width), F32)] * n_out,
        compiler_params=_cparams("parallel"), name="headnorm")(*args)
    return outs[0] if n_out == 1 else outs


def _dn_gates(ba, pa_ref, pd_ref):
    beta = _sigmoid(ba)
    g = -jnp.exp(pa_ref[...]) * _softplus(ba + pd_ref[...])
    return beta, g


def _dna_kernel(qkv_ref, halo_ref, ba_ref, cw_ref, pa_ref, pd_ref,
                w_o, u_o, qg_o, kd_o, qk_o, gc_o, *, C):
    n = pl.program_id(1)
    x = qkv_ref[0]
    halo = jnp.where(n == 0, 0.0, halo_ref[0])
    c = (x * cw_ref[3:4, :] + _shift_rows(x, halo, 1) * cw_ref[2:3, :]
         + _shift_rows(x, halo, 2) * cw_ref[1:2, :] + _shift_rows(x, halo, 3) * cw_ref[0:1, :])
    c = _silu(c)
    beta_all, gmat = _dn_gates(ba_ref[0], pa_ref, pd_ref)
    H, dk = DN_HEADS, DN_DK
    HC = H * C
    ri = lax.broadcasted_iota(jnp.int32, (C, C), 0)
    ci = lax.broadcasted_iota(jnp.int32, (C, C), 1)
    gc = _dot(jnp.where(ri >= ci, 1.0, 0.0), gmat, HI)
    gc_o[0] = gc
    qs, ks, vs, bs, gs = [], [], [], [], []
    for h in range(H):
        qh = c[:, h * dk:(h + 1) * dk]
        kh = c[:, (H + h) * dk:(H + h + 1) * dk]
        qs.append(qh * lax.rsqrt(jnp.sum(qh * qh, axis=-1, keepdims=True) + EPS) * (dk ** -0.5))
        ks.append(kh * lax.rsqrt(jnp.sum(kh * kh, axis=-1, keepdims=True) + EPS))
        vs.append(c[:, (2 * H + h) * dk:(2 * H + h + 1) * dk])
        bs.append(beta_all[:, h:h + 1])
        gs.append(gc[:, H + h:H + h + 1])
    qst = jnp.concatenate(qs, axis=0)
    kst = jnp.concatenate(ks, axis=0)
    vst = jnp.concatenate(vs, axis=0)
    bst = jnp.concatenate(bs, axis=0)
    gcol = jnp.concatenate(gs, axis=0)
    lane0 = lax.broadcasted_iota(jnp.int32, (HC, LANES), 1) == 0
    e0 = jnp.where((lax.broadcasted_iota(jnp.int32, (SUBLANES, LANES), 0) == 0)
                   & (lax.broadcasted_iota(jnp.int32, (SUBLANES, LANES), 1) == 0), 1.0, 0.0)
    grow = _nt(e0, jnp.where(lane0, gcol, 0.0), HI)[0:1]
    rr = lax.broadcasted_iota(jnp.int32, (HC, HC), 0)
    cc = lax.broadcasted_iota(jnp.int32, (HC, HC), 1)
    same = (rr // C) == (cc // C)
    lower = same & (rr >= cc)
    decay = jnp.where(lower, jnp.exp(jnp.where(lower, gcol - grow, 0.0)), 0.0)
    kbst = kst * bst
    lmat = jnp.where(same & (rr > cc), _nt3(kbst, kst) * decay, 0.0)
    tinv = jnp.where(rr == cc, 1.0, 0.0) - lmat
    n_fac = max(int(np.ceil(np.log2(C))) - 1, 0)
    if n_fac > 0:
        p = _dot3(lmat, lmat)
        for f in range(n_fac):
            tinv = tinv + _dot3(tinv, p)
            if f + 1 < n_fac:
                p = _dot3(p, p)
    eg = jnp.exp(gcol)
    wu = _dot3(tinv, jnp.concatenate([kbst * eg, vst * bst], axis=1))
    qk = jnp.where(lower, _nt3(qst, kst) * decay, 0.0)
    qk_sum = qk[0:C]
    for h in range(1, H):
        qk_sum = qk_sum + qk[h * C:(h + 1) * C]
    qk_o[0] = qk_sum
    for h in range(H):
        sl = slice(h * dk, (h + 1) * dk)
        rows = slice(h * C, (h + 1) * C)
        w_o[0, :, sl] = wu[rows, :dk]
        u_o[0, :, sl] = wu[rows, dk:]
        qg_o[0, :, sl] = qs[h] * eg[rows]
        kd_o[0, :, sl] = ks[h] * jnp.exp(gs[h][C - 1:C, :] - gs[h])


def _dn_prepare(qkv3, ba3, cw8, pa, pd):
    b, t, wq = qkv3.shape
    C = DN_CHUNK
    assert t % C == 0
    nh = DN_HEADS * DN_DK
    hb = C // SUBLANES
    outs = pl.pallas_call(
        functools.partial(_dna_kernel, C=C),
        grid=(b, t // C),
        in_specs=[pl.BlockSpec((1, C, wq), lambda i, n: (i, n, 0)),
                  pl.BlockSpec((1, SUBLANES, wq), lambda i, n: (i, jnp.maximum(n * hb - 1, 0), 0)),
                  pl.BlockSpec((1, C, LANES), lambda i, n: (i, n, 0)),
                  pl.BlockSpec((SUBLANES, wq), lambda i, n: (0, 0)),
                  pl.BlockSpec((1, LANES), lambda i, n: (0, 0)),
                  pl.BlockSpec((1, LANES), lambda i, n: (0, 0))],
        out_specs=[pl.BlockSpec((1, C, nh), lambda i, n: (i, n, 0))] * 4
        + [pl.BlockSpec((1, C, DN_HEADS * C), lambda i, n: (i, n, 0)),
           pl.BlockSpec((1, C, LANES), lambda i, n: (i, n, 0))],
        out_shape=[jax.ShapeDtypeStruct((b, t, nh), F32)] * 4
        + [jax.ShapeDtypeStruct((b, t, DN_HEADS * C), F32), jax.ShapeDtypeStruct((b, t, LANES), F32)],
        compiler_params=_cparams("parallel", "parallel"), name="dn_prepare")(qkv3, qkv3, ba3, cw8, pa, pd)
    return outs


def _dnb_kernel(w_ref, u_ref, qg_ref, kd_ref, qk_ref, gc_ref, z_ref, gn_ref, o_ref, s_out, s_ref, *, C, nb):
    n = pl.program_id(0)

    @pl.when(n == 0)
    def _():
        s_ref[...] = jnp.zeros_like(s_ref)

    dk = DN_DK
    for b in range(nb):
        for h in range(DN_HEADS):
            sl = slice(h * dk, (h + 1) * dk)
            s = s_ref[b * DN_HEADS + h]
            v_new = u_ref[b, :, sl] - _dot3(w_ref[b, :, sl], s)
            o = _dot3(qg_ref[b, :, sl], s) + _dot3(qk_ref[b, :, h * C:(h + 1) * C], v_new)
            dl = jnp.exp(gc_ref[b, C - 1:C, DN_HEADS + h:DN_HEADS + h + 1])
            s_ref[b * DN_HEADS + h] = s * dl + _tn3(kd_ref[b, :, sl], v_new)
            o = o * lax.rsqrt(jnp.mean(o * o, axis=-1, keepdims=True) + EPS) * gn_ref[...]
            o_ref[b, :, sl] = o * _silu(z_ref[b, :, sl])

    @pl.when(n == pl.num_programs(0) - 1)
    def _():
        for b in range(nb):
            for h in range(DN_HEADS):
                s_out[b, h] = s_ref[b * DN_HEADS + h]


def _dn_recur(w, u, qg, kd, qk, gc, z3, gn):
    b, t, nh = w.shape
    C = DN_CHUNK
    big = pl.BlockSpec((b, C, nh), lambda n: (0, n, 0))
    return pl.pallas_call(
        functools.partial(_dnb_kernel, C=C, nb=b),
        grid=(t // C,),
        in_specs=[big, big, big, big,
                  pl.BlockSpec((b, C, DN_HEADS * C), lambda n: (0, n, 0)),
                  pl.BlockSpec((b, C, LANES), lambda n: (0, n, 0)),
                  big, pl.BlockSpec((1, DN_DK), lambda n: (0, 0))],
        out_specs=[big, pl.BlockSpec((b, DN_HEADS, DN_DK, DN_DK), lambda n: (0, 0, 0, 0))],
        out_shape=[jax.ShapeDtypeStruct((b, t, nh), F32),
                   jax.ShapeDtypeStruct((b, DN_HEADS, DN_DK, DN_DK), F32)],
        scratch_shapes=[pltpu.VMEM((b * DN_HEADS, DN_DK, DN_DK), F32)],
        compiler_params=_cparams("arbitrary"), name="dn_recur")(w, u, qg, kd, qk, gc, z3, gn)


def _dns_kernel(qkv_ref, st_ref, ba_ref, z_ref, s_ref, cw_ref, pa_ref, pd_ref, gn_ref,
                o_ref, so_ref, sto_ref):
    new = qkv_ref[0]
    st = st_ref[0]
    c = (st[0:1] * cw_ref[0:1, :] + st[1:2] * cw_ref[1:2, :] + st[2:3] * cw_ref[2:3, :]
         + new * cw_ref[3:4, :])
    c = _silu(c)
    sto_ref[0, 0:2, :] = st[1:3]
    sto_ref[0, 2:3, :] = new
    beta_all, gmat = _dn_gates(ba_ref[0], pa_ref, pd_ref)
    dk = DN_DK
    row = lax.broadcasted_iota(jnp.int32, (SUBLANES, dk), 0)
    for h in range(DN_HEADS):
        qh = c[:, h * dk:(h + 1) * dk]
        kh = c[:, (DN_HEADS + h) * dk:(DN_HEADS + h + 1) * dk]
        vh = c[:, (2 * DN_HEADS + h) * dk:(2 * DN_HEADS + h + 1) * dk]
        qh = qh * lax.rsqrt(jnp.sum(qh * qh, axis=-1, keepdims=True) + EPS) * (dk ** -0.5)
        kh = kh * lax.rsqrt(jnp.sum(kh * kh, axis=-1, keepdims=True) + EPS)
        bh = beta_all[:, h:h + 1]
        eg = jnp.exp(gmat[:, DN_HEADS + h:DN_HEADS + h + 1])
        s = s_ref[0, h]
        kq = jnp.where(row == 0, kh, jnp.where(row == 1, qh, 0.0))
        ks_qs = _dot(kq, s, HI)
        v_new = bh * (vh - eg * ks_qs[0:1])
        o = eg * ks_qs[1:2] + jnp.sum(qh * kh, axis=-1, keepdims=True) * v_new
        k8 = jnp.where(row == 0, kh, 0.0)
        v8 = jnp.where(row == 0, v_new, 0.0)
        so_ref[0, h] = s * eg + _tn(k8, v8, HI)
        o = o * lax.rsqrt(jnp.mean(o * o, axis=-1, keepdims=True) + EPS) * gn_ref[...]
        o_ref[0, :, h * dk:(h + 1) * dk] = o * _silu(z_ref[0, :, h * dk:(h + 1) * dk])


def _dn_step(qkv, conv_state, ba, z, state, cw8, pa, pd, gn):
    b, wq = qkv.shape
    nh = DN_HEADS * DN_DK
    one = lambda i: (i, 0, 0)
    cst = lambda i: (0, 0)
    return pl.pallas_call(
        _dns_kernel, grid=(b,),
        in_specs=[pl.BlockSpec((1, 1, wq), one), pl.BlockSpec((1, 3, wq), one),
                  pl.BlockSpec((1, 1, LANES), one), pl.BlockSpec((1, 1, nh), one),
                  pl.BlockSpec((1, DN_HEADS, DN_DK, DN_DK), lambda i: (i, 0, 0, 0)),
                  pl.BlockSpec((SUBLANES, wq), cst), pl.BlockSpec((1, LANES), cst),
                  pl.BlockSpec((1, LANES), cst), pl.BlockSpec((1, DN_DK), cst)],
        out_specs=[pl.BlockSpec((1, 1, nh), one),
                   pl.BlockSpec((1, DN_HEADS, DN_DK, DN_DK), lambda i: (i, 0, 0, 0)),
                   pl.BlockSpec((1, 3, wq), one)],
        out_shape=[jax.ShapeDtypeStruct((b, 1, nh), F32),
                   jax.ShapeDtypeStruct(state.shape, F32),
                   jax.ShapeDtypeStruct(conv_state.shape, F32)],
        compiler_params=_cparams("parallel"), name="dn_step")(
            qkv.reshape(b, 1, wq), conv_state, ba.reshape(b, 1, LANES), z.reshape(b, 1, nh),
            state, cw8, pa, pd, gn)


def _band_kernel(*refs, H, G, window, tq, span, T, has_sink, gate_col):
    q_ref, k_ref, v_ref = refs[:3]
    i = 3
    sink_ref = gate_ref = None
    if has_sink:
        sink_ref = refs[i]
        i += 1
    if gate_col is not None:
        gate_ref = refs[i]
        i += 1
    o_ref = refs[i]
    qi = pl.program_id(1)
    start = jnp.minimum(jnp.maximum(qi * tq - window, 0), T - span)
    start = pl.multiple_of(start, tq)
    kb = k_ref[0, pl.ds(start, span), :]
    vb = v_ref[0, pl.ds(start, span), :]
    R = H // G
    hd = HEAD_DIM
    scale = hd ** -0.5
    qpos = qi * tq + lax.broadcasted_iota(jnp.int32, (span, tq), 1)
    kpos = start + lax.broadcasted_iota(jnp.int32, (span, tq), 0)
    d = qpos - kpos
    neg = jnp.where((d >= 0) & (d < window), 0.0, NEG_INF)
    neg = jnp.concatenate([neg] * R, axis=1)
    pad = jnp.zeros((LANES - hd, tq), F32)
    for g in range(G):
        kg = kb[:, g * hd:(g + 1) * hd].astype(BF16)
        vg = vb[:, g * hd:(g + 1) * hd].astype(BF16)
        qs = jnp.concatenate([(q_ref[0, :, (g * R + r) * hd:(g * R + r + 1) * hd] * scale).astype(BF16)
                              for r in range(R)], axis=0)
        st = _nt(kg, qs) + neg
        m = jnp.max(st, axis=0, keepdims=True)
        if has_sink:
            sk = jnp.concatenate([jnp.broadcast_to(sink_ref[0:1, g * R + r:g * R + r + 1], (1, tq))
                                  for r in range(R)], axis=1)
            m = jnp.maximum(m, sk)
            e = jnp.exp(st - m)
            den = jnp.sum(e, axis=0, keepdims=True) + jnp.exp(sk - m)
        else:
            m = jnp.where(m == NEG_INF, 0.0, m)
            e = jnp.exp(st - m)
            den = jnp.maximum(jnp.sum(e, axis=0, keepdims=True), 1e-30)
        acc = _tn(vg, e.astype(BF16)) / den
        for r in range(R):
            h = g * R + r
            o = jnp.concatenate([acc[:, r * tq:(r + 1) * tq], pad], axis=0).T[:, :hd]
            if gate_ref is not None:
                o = o * _sigmoid(gate_ref[0, :, 3 * h + gate_col:3 * h + gate_col + 1])
            o_ref[0, :, h * hd:(h + 1) * hd] = o


def _band_attn(q3, k3, v3, *, H, G, window, sinks=None, gate3=None, gate_col=None):
    b, t, _ = q3.shape
    tq = min(QBLK, t)
    assert t % tq == 0
    span = min(window + tq, t)
    in_specs = [pl.BlockSpec((1, tq, H * HEAD_DIM), lambda i, j: (i, j, 0)),
                pl.BlockSpec((1, t, G * HEAD_DIM), lambda i, j: (i, 0, 0)),
                pl.BlockSpec((1, t, G * HEAD_DIM), lambda i, j: (i, 0, 0))]
    args = [q3, k3, v3]
    if sinks is not None:
        in_specs.append(pl.BlockSpec((1, LANES), lambda i, j: (0, 0)))
        args.append(jnp.pad(sinks.astype(F32), (0, LANES - H)).reshape(1, LANES))
    if gate3 is not None:
        in_specs.append(pl.BlockSpec((1, tq, LANES), lambda i, j: (i, j, 0)))
        args.append(gate3)
    else:
        gate_col = None
    return pl.pallas_call(
        functools.partial(_band_kernel, H=H, G=G, window=window, tq=tq, span=span, T=t,
                          has_sink=sinks is not None, gate_col=gate_col),
        grid=(b, t // tq), in_specs=in_specs,
        out_specs=pl.BlockSpec((1, tq, H * HEAD_DIM), lambda i, j: (i, j, 0)),
        out_shape=jax.ShapeDtypeStruct((b, t, H * HEAD_DIM), F32),
        compiler_params=_cparams("parallel", "parallel"), name="band_attn")(*args)


def _swin_kernel(*refs, H, G, W, has_sink, gate_col):
    q_ref, ck_ref, cv_ref, nk_ref, nv_ref = refs[:5]
    i = 5
    sink_ref = gate_ref = None
    if has_sink:
        sink_ref = refs[i]
        i += 1
    if gate_col is not None:
        gate_ref = refs[i]
        i += 1
    ok_ref, ov_ref, o_ref = refs[i:i + 3]
    row = lax.broadcasted_iota(jnp.int32, (W, G * HEAD_DIM), 0)
    kn = jnp.where(row == W - 1, nk_ref[0], pltpu.roll(ck_ref[0], W - 1, 0))
    vn = jnp.where(row == W - 1, nv_ref[0], pltpu.roll(cv_ref[0], W - 1, 0))
    ok_ref[0] = kn
    ov_ref[0] = vn
    R = H // G
    hd = HEAD_DIM
    scale = hd ** -0.5
    for g in range(G):
        kg = kn[:, g * hd:(g + 1) * hd].astype(BF16)
        vg = vn[:, g * hd:(g + 1) * hd].astype(BF16)
        qg = (q_ref[0, g * R:(g + 1) * R, :] * scale).astype(BF16)
        s = _nt(qg, kg)
        m = jnp.max(s, axis=-1, keepdims=True)
        if has_sink:
            sk = sink_ref[g * R:(g + 1) * R, :]
            m = jnp.maximum(m, sk)
            e = jnp.exp(s - m)
            den = jnp.sum(e, axis=-1, keepdims=True) + jnp.exp(sk - m)
        else:
            e = jnp.exp(s - m)
            den = jnp.maximum(jnp.sum(e, axis=-1, keepdims=True), 1e-30)
        o = _dot((e / den).astype(BF16), vg)
        if gate_ref is not None:
            o = o * _sigmoid(gate_ref[0, g * R:(g + 1) * R, gate_col:gate_col + 1])
        o_ref[0, g * R:(g + 1) * R, :] = o


def _step_window_attn(q, cache_k, cache_v, new_k, new_v, *, H, G, sinks=None, gate=None, gate_col=None):
    b, W, gd = cache_k.shape
    one = lambda i: (i, 0, 0)
    in_specs = [pl.BlockSpec((1, H, HEAD_DIM), one), pl.BlockSpec((1, W, gd), one),
                pl.BlockSpec((1, W, gd), one), pl.BlockSpec((1, 1, gd), one), pl.BlockSpec((1, 1, gd), one)]
    args = [q, cache_k, cache_v, new_k.reshape(b, 1, gd), new_v.reshape(b, 1, gd)]
    if sinks is not None:
        in_specs.append(pl.BlockSpec((H, 1), lambda i: (0, 0)))
        args.append(sinks.astype(F32).reshape(H, 1))
    if gate is not None:
        in_specs.append(pl.BlockSpec((1, H, 3), one))
        args.append(gate)
    else:
        gate_col = None
    return pl.pallas_call(
        functools.partial(_swin_kernel, H=H, G=G, W=W, has_sink=sinks is not None, gate_col=gate_col),
        grid=(b,), in_specs=in_specs,
        out_specs=[pl.BlockSpec((1, W, gd), one), pl.BlockSpec((1, W, gd), one),
                   pl.BlockSpec((1, H, HEAD_DIM), one)],
        out_shape=[jax.ShapeDtypeStruct((b, W, gd), F32), jax.ShapeDtypeStruct((b, W, gd), F32),
                   jax.ShapeDtypeStruct((b, H, HEAD_DIM), F32)],
        compiler_params=_cparams("parallel"), name="step_window_attn")(*args)


def _cmp_core(load_rows, pe_ref, wp_ref, w2_ref, gain_ref, out_ref, *, r0, rd):
    lane = lax.broadcasted_iota(jnp.int32, (1, LANES), 1)
    out = jnp.zeros((r0, NSA_KV_HEADS * HEAD_DIM), F32)
    for gp in range(2):
        acc = jnp.zeros((rd + SUBLANES, 2 * LANES), F32)
        for ip in range(CMP_STRIDE // 2):
            ab = jnp.concatenate([load_rows(2 * ip, gp), load_rows(2 * ip + 1, gp)], axis=1)
            lhs = jnp.concatenate([ab, pe_ref[ip]], axis=0).astype(BF16)
            acc = acc + _dot(lhs, wp_ref[ip])
        for gl in range(2):
            piece = acc[:, gl * LANES:(gl + 1) * LANES]
            bias = jnp.where(lane < HEAD_DIM, piece[rd:rd + 1], piece[rd + 1:rd + 2])
            data = piece[:rd] + bias
            pre = data + pltpu.roll(pltpu.roll(data, rd - 1, 0), HEAD_DIM, 1)
            hcat = _silu(pre)[:r0]
            out = out + _dot(hcat.astype(BF16), w2_ref[2 * gp + gl])
    if gain_ref is not None:
        n = NSA_KV_HEADS * HEAD_DIM
        r = lax.broadcasted_iota(jnp.int32, (n, n), 0) // HEAD_DIM
        c = lax.broadcasted_iota(jnp.int32, (n, n), 1) // HEAD_DIM
        bd = jnp.where(r == c, 1.0, 0.0).astype(BF16)
        x2 = out * out
        hi = x2.astype(BF16)
        lo = (x2 - hi.astype(F32)).astype(BF16)
        ss = _dot(hi, bd) + _dot(lo, bd)
        out = out * lax.rsqrt(ss * (1.0 / HEAD_DIM) + EPS) * gain_ref[...]
    out_ref[0] = out


def _cmp_seq_kernel(*refs, r0, has_norm):
    x_refs, (pe_ref, wp_ref, w2_ref) = refs[:2], refs[2:5]
    gain_ref = refs[5] if has_norm else None
    out_ref = refs[-1]

    def load_rows(i, gp):
        return x_refs[gp][0, pl.ds(i, r0, stride=CMP_STRIDE), :]

    _cmp_core(load_rows, pe_ref, wp_ref, w2_ref, gain_ref, out_ref, r0=r0, rd=r0)


def _cmp_paged_kernel(*refs, r0, n_pages, page, has_norm):
    pages = refs[1:1 + n_pages + 1]
    i = n_pages + 2
    pe_ref, wp_ref, w2_ref = refs[i:i + 3]
    gain_ref = refs[i + 3] if has_norm else None
    out_ref = refs[-5]
    bufs = (refs[-4:-2], refs[-2:])
    g = pl.program_id(0)
    rd = r0 + SUBLANES

    @pl.when(g == 0)
    def _():
        for ref in bufs[1]:
            ref[...] = jnp.zeros(ref.shape, F32)

    def step(fill, drain):
        for j, p in enumerate(pages):
            for gp in range(2):
                fill[gp][j * page:(j + 1) * page, :] = p[0, gp * LANES:(gp + 1) * LANES, :].T

        def load_rows(i, gp):
            return drain[gp][pl.ds(i, rd, stride=CMP_STRIDE), :]

        _cmp_core(load_rows, pe_ref, wp_ref, w2_ref, gain_ref, out_ref, r0=r0, rd=rd)

    @pl.when(g % 2 == 0)
    def _():
        step(bufs[0], bufs[1])

    @pl.when(g % 2 == 1)
    def _():
        step(bufs[1], bufs[0])


def _cmp_weights(pe, w1, w2):
    hd, G = HEAD_DIM, NSA_KV_HEADS
    npair = CMP_STRIDE // 2
    w1r = w1.reshape(2, npair, 2, hd, hd)
    wp = jnp.einsum('ab,hpidn->piadbhn', jnp.eye(2, dtype=F32), w1r)
    wp = wp.reshape(npair, 4 * hd, 4 * hd).astype(BF16)
    w2g = jnp.zeros((G, LANES, G * hd), F32)
    for g in range(G):
        w2g = w2g.at[g, :hd, g * hd:(g + 1) * hd].set(w2)
    per = jnp.broadcast_to(pe.reshape(2, npair, 2, 1, hd), (2, npair, 2, 2, hd))
    per = per.transpose(1, 0, 2, 3, 4).reshape(npair, 2, 4 * hd)
    pe_rows = jnp.pad(per.astype(F32), ((0, 0), (0, SUBLANES - 2), (0, 0)))
    return pe_rows, wp, w2g.astype(BF16)


def _cmp_specs(has_norm):
    gd = NSA_KV_HEADS * HEAD_DIM
    specs = [pl.BlockSpec((CMP_STRIDE // 2, SUBLANES, 4 * HEAD_DIM), lambda *a: (0, 0, 0)),
             pl.BlockSpec((CMP_STRIDE // 2, 4 * HEAD_DIM, 4 * HEAD_DIM), lambda *a: (0, 0, 0)),
             pl.BlockSpec((NSA_KV_HEADS, LANES, gd), lambda *a: (0, 0, 0))]
    if has_norm:
        specs.append(pl.BlockSpec((1, gd), lambda *a: (0, 0)))
    return specs


def _compress_seq(x3, pe, w1, w2, gain=None):
    b, t, gd = x3.shape
    r0 = t // CMP_STRIDE
    pe_rows, wp, w2g = _cmp_weights(pe, w1, w2)
    assert gd == 2 * LANES
    args = [x3, x3, pe_rows, wp, w2g]
    if gain is not None:
        args.append(jnp.tile(gain.reshape(1, HEAD_DIM), (1, NSA_KV_HEADS)))
    return pl.pallas_call(
        functools.partial(_cmp_seq_kernel, r0=r0, has_norm=gain is not None),
        grid=(b,),
        in_specs=[pl.BlockSpec((1, t, LANES), lambda i: (i, 0, 0)), pl.BlockSpec((1, t, LANES), lambda i: (i, 0, 1))]
        + _cmp_specs(gain is not None),
        out_specs=pl.BlockSpec((1, r0, gd), lambda i: (i, 0, 0)),
        out_shape=jax.ShapeDtypeStruct((b, r0, gd), F32),
        compiler_params=_cparams("parallel"), name="compress_seq")(*args)


def _pages_gd_tok(cache):
    n_pool, page, G, hd = cache.shape
    return jnp.transpose(cache, (0, 2, 3, 1)).reshape(n_pool, G * hd, page)


def _compress_paged(cache, page_table, pe, w1, w2, gain=None):
    n_pool, page, G, hd = cache.shape
    gd = G * hd
    b, n_pages = page_table.shape
    assert page == LANES and gd == 2 * LANES and n_pages % PAGES_PER_STEP == 0
    r0 = PAGES_PER_STEP * page // CMP_STRIDE
    nq = n_pages // PAGES_PER_STEP
    cv = _pages_gd_tok(cache)
    pe_rows, wp, w2g = _cmp_weights(pe, w1, w2)

    n_steps = b * nq
    n_in = PAGES_PER_STEP + 1
    step_page = jnp.minimum(jnp.arange(nq)[:, None] * PAGES_PER_STEP + jnp.arange(n_in)[None, :], n_pages - 1)
    flat = page_table[:, step_page].reshape(n_steps, n_in)
    flat = jnp.concatenate([flat, flat[-1:]], axis=0).reshape(-1).astype(jnp.int32)

    def page_spec(j):
        return pl.BlockSpec((1, gd, page), lambda g, pt: (pt[g * n_in + j], 0, 0))

    def out_map(g, pt):
        gm = jnp.maximum(g - 1, 0)
        return (gm // nq, gm % nq, 0)

    in_specs = [page_spec(j) for j in range(PAGES_PER_STEP + 1)] + _cmp_specs(gain is not None)
    args = [cv] * (PAGES_PER_STEP + 1) + [pe_rows, wp, w2g]
    if gain is not None:
        args.append(jnp.tile(gain.reshape(1, HEAD_DIM), (1, NSA_KV_HEADS)))
    gs = pltpu.PrefetchScalarGridSpec(
        num_scalar_prefetch=1, grid=(n_steps + 1,), in_specs=in_specs,
        out_specs=pl.BlockSpec((1, r0, gd), out_map),
        scratch_shapes=[pltpu.VMEM(((PAGES_PER_STEP + 1) * page, LANES), F32)] * 4)
    return pl.pallas_call(
        functools.partial(_cmp_paged_kernel, r0=r0, n_pages=PAGES_PER_STEP, page=page, has_norm=gain is not None),
        grid_spec=gs, out_shape=jax.ShapeDtypeStruct((b, nq * r0, gd), F32),
        compiler_params=_cparams("arbitrary"), name="compress_paged")(flat, *args)


def _rank_rows(imp_ref, n_cand):
    shape = imp_ref.shape
    x = imp_ref[...]
    ridx = lax.broadcasted_iota(jnp.int32, shape, 0)

    def body(sp, cnt):
        row = imp_ref[pl.ds(sp, 1), :]
        ge = jnp.where(row >= x, 1.0, 0.0)
        gt = jnp.where(row > x, 1.0, 0.0)
        return cnt + jnp.where(sp < ridx, ge, gt)

    return lax.fori_loop(0, n_cand, body, jnp.zeros(shape, F32))


def _rank_kernel(imp_ref, rank_ref, *, n_cand):
    rank_ref[...] = _rank_rows(imp_ref, n_cand)


def _rank_call(imp_t, n_cand):
    return pl.pallas_call(
        functools.partial(_rank_kernel, n_cand=n_cand),
        out_shape=jax.ShapeDtypeStruct(imp_t.shape, F32),
        compiler_params=pltpu.CompilerParams(vmem_limit_bytes=VMEM_LIMIT), name="rank")(imp_t)


def _cmpattn_kernel(q_ref, kc_ref, vc_ref, gate_ref, cov_ref, o_ref, sel_ref, imp_ref,
                    *, tq, n_cmp, n_sel, k_top):
    qi = pl.program_id(1)
    nc = kc_ref.shape[1]
    ns = cov_ref.shape[0]
    hd = HEAD_DIM
    G, R = NSA_KV_HEADS, NSA_HEADS // NSA_KV_HEADS
    scale = hd ** -0.5
    t_q = qi * tq + lax.broadcasted_iota(jnp.int32, (nc, tq), 1)
    cidx = lax.broadcasted_iota(jnp.int32, (nc, tq), 0)
    neg = jnp.where((cidx * CMP_STRIDE + CMP_BLK - 1 <= t_q) & (cidx < n_cmp), 0.0, NEG_INF)
    neg = jnp.concatenate([neg] * R, axis=1)
    sid = lax.broadcasted_iota(jnp.int32, (ns, tq), 0)
    tt = qi * tq + lax.broadcasted_iota(jnp.int32, (ns, tq), 1)
    bt = tt // SEL_BLK
    forced = (sid == 0) | (sid == bt) | (sid == bt - 1)
    valid = sid * SEL_BLK <= tt
    pad = jnp.zeros((LANES - hd, tq), F32)
    for g in range(G):
        kg = kc_ref[0, :, g * hd:(g + 1) * hd].astype(BF16)
        vg = vc_ref[0, :, g * hd:(g + 1) * hd].astype(BF16)
        qs = jnp.concatenate([(q_ref[0, :, (g * R + r) * hd:(g * R + r + 1) * hd] * scale).astype(BF16)
                              for r in range(R)], axis=0)
        st = _nt(kg, qs) + neg
        m = jnp.max(st, axis=0, keepdims=True)
        m = jnp.where(m == NEG_INF, 0.0, m)
        e = jnp.exp(st - m)
        p = e / jnp.maximum(jnp.sum(e, axis=0, keepdims=True), 1e-30)
        acc = _tn(vg, p.astype(BF16))
        psum = p[:, 0:tq]
        for r in range(1, R):
            psum = psum + p[:, r * tq:(r + 1) * tq]
        for r in range(R):
            h = g * R + r
            o = jnp.concatenate([acc[:, r * tq:(r + 1) * tq], pad], axis=0).T[:, :hd]
            o_ref[0, :, h * hd:(h + 1) * hd] = o * _sigmoid(gate_ref[0, :, 3 * h:3 * h + 1])
        p1, p2, p3 = _split3(psum)
        imp = _dot(cov_ref[...], p1) + _dot(cov_ref[...], p2) + _dot(cov_ref[...], p3)
        imp_ref[:, g * tq:(g + 1) * tq] = jnp.where(forced, SEL_BONUS, jnp.where(valid, imp, -SEL_BONUS))
    rank = _rank_rows(imp_ref, n_sel)
    for g in range(G):
        sel_ref[0, 0, g] = jnp.where(rank[:, g * tq:(g + 1) * tq] < k_top, 1.0, 0.0)


def _cover(n_cmp_rows, n_sel_cols):
    cstart = np.arange(n_cmp_rows)[:, None] * CMP_STRIDE
    sstart = np.arange(n_sel_cols)[None, :] * SEL_BLK
    return ((cstart < sstart + SEL_BLK) & (cstart + CMP_BLK > sstart)).astype(np.float32)


def _cmp_attn_seq(q3, kcmp, vcmp, gate3):
    b, t, _ = q3.shape
    tq = min(QBLK, t)
    nc = kcmp.shape[1]
    n_cmp = (t - CMP_BLK) // CMP_STRIDE + 1
    n_sel = -(-t // SEL_BLK)
    k_top = min(N_SEL, n_sel)
    cov_t = jnp.asarray(_cover(nc, n_sel).T, BF16)
    gd = NSA_KV_HEADS * HEAD_DIM
    return pl.pallas_call(
        functools.partial(_cmpattn_kernel, tq=tq, n_cmp=n_cmp, n_sel=n_sel, k_top=k_top),
        grid=(b, t // tq),
        in_specs=[pl.BlockSpec((1, tq, NSA_HEADS * HEAD_DIM), lambda i, j: (i, j, 0)),
                  pl.BlockSpec((1, nc, gd), lambda i, j: (i, 0, 0)),
                  pl.BlockSpec((1, nc, gd), lambda i, j: (i, 0, 0)),
                  pl.BlockSpec((1, tq, LANES), lambda i, j: (i, j, 0)),
                  pl.BlockSpec((n_sel, nc), lambda i, j: (0, 0))],
        out_specs=[pl.BlockSpec((1, tq, NSA_HEADS * HEAD_DIM), lambda i, j: (i, j, 0)),
                   pl.BlockSpec((1, 1, NSA_KV_HEADS, n_sel, tq), lambda i, j: (i, j, 0, 0, 0))],
        out_shape=[jax.ShapeDtypeStruct((b, t, NSA_HEADS * HEAD_DIM), F32),
                   jax.ShapeDtypeStruct((b, t // tq, NSA_KV_HEADS, n_sel, tq), F32)],
        scratch_shapes=[pltpu.VMEM((n_sel, NSA_KV_HEADS * tq), F32)],
        compiler_params=_cparams("parallel", "parallel"), name="cmp_attn_seq")(q3, kcmp, vcmp, gate3, cov_t)


def _cmpstep_kernel(q_ref, kc_ref, vc_ref, gate_ref, cov_ref, o_ref, imp_ref, *, n_cmp, n_sel, pos):
    nc = kc_ref.shape[1]
    ns = cov_ref.shape[1]
    hd = HEAD_DIM
    G, R = NSA_KV_HEADS, NSA_HEADS // NSA_KV_HEADS
    scale = hd ** -0.5
    cidx = lax.broadcasted_iota(jnp.int32, (R, nc), 1)
    mask = (cidx * CMP_STRIDE + CMP_BLK - 1 <= pos) & (cidx < n_cmp)
    psums = []
    for g in range(G):
        kg = kc_ref[0, :, g * hd:(g + 1) * hd].astype(BF16)
        vg = vc_ref[0, :, g * hd:(g + 1) * hd].astype(BF16)
        qg = (q_ref[0, g * R:(g + 1) * R, :] * scale).astype(BF16)
        s = jnp.where(mask, _nt(qg, kg), NEG_INF)
        m = jnp.max(s, axis=-1, keepdims=True)
        m = jnp.where(m == NEG_INF, 0.0, m)
        e = jnp.exp(s - m)
        p = e / jnp.maximum(jnp.sum(e, axis=-1, keepdims=True), 1e-30)
        o = _dot(p.astype(BF16), vg)
        o_ref[0, g * R:(g + 1) * R, :] = o * _sigmoid(gate_ref[0, g * R:(g + 1) * R, 0:1])
        psums.append(jnp.sum(p, axis=0, keepdims=True))
    p1, p2, p3 = _split3(jnp.concatenate(psums, axis=0))
    imp = _dot(p1, cov_ref[...]) + _dot(p2, cov_ref[...]) + _dot(p3, cov_ref[...])
    sid = lax.broadcasted_iota(jnp.int32, (G, ns), 1)
    bt = pos // SEL_BLK
    forced = (sid == 0) | (sid == bt) | (sid == bt - 1)
    valid = sid * SEL_BLK <= pos
    imp_ref[0] = jnp.where(forced, SEL_BONUS, jnp.where(valid, imp, -SEL_BONUS))


def _cmp_attn_step(q3, kcmp, vcmp, gate3, *, pos, n_cmp, n_sel, ns_pad):
    b = q3.shape[0]
    nc = kcmp.shape[1]
    cov = jnp.asarray(_cover(nc, ns_pad), BF16)
    gd = NSA_KV_HEADS * HEAD_DIM
    one = lambda i: (i, 0, 0)
    return pl.pallas_call(
        functools.partial(_cmpstep_kernel, n_cmp=n_cmp, n_sel=n_sel, pos=pos),
        grid=(b,),
        in_specs=[pl.BlockSpec((1, NSA_HEADS, HEAD_DIM), one), pl.BlockSpec((1, nc, gd), one),
                  pl.BlockSpec((1, nc, gd), one), pl.BlockSpec((1, NSA_HEADS, 3), one),
                  pl.BlockSpec((nc, ns_pad), lambda i: (0, 0))],
        out_specs=[pl.BlockSpec((1, NSA_HEADS, HEAD_DIM), one), pl.BlockSpec((1, NSA_KV_HEADS, ns_pad), one)],
        out_shape=[jax.ShapeDtypeStruct((b, NSA_HEADS, HEAD_DIM), F32),
                   jax.ShapeDtypeStruct((b, NSA_KV_HEADS, ns_pad), F32)],
        compiler_params=_cparams("parallel"), name="cmp_attn_step")(q3, kcmp, vcmp, gate3, cov)


def _selattn_kernel(q_ref, k_ref, v_ref, sel_ref, gate_ref, o_ref, qs_ref, m_ref, l_ref, acc_ref,
                    *, tq, tk, n_sel):
    qi = pl.program_id(1)
    hd = HEAD_DIM
    G, R = NSA_KV_HEADS, NSA_HEADS // NSA_KV_HEADS
    scale = hd ** -0.5
    n_tiles = ((qi + 1) * tq + tk - 1) // tk
    tt = qi * tq + lax.broadcasted_iota(jnp.int32, (tk, tq), 1)
    krow = lax.broadcasted_iota(jnp.int32, (tk, tq), 0)
    pad = jnp.zeros((LANES - hd, tq), F32)
    for g in range(G):
        for r in range(R):
            h = g * R + r
            qs_ref[r * tq:(r + 1) * tq, :] = (q_ref[0, :, h * hd:(h + 1) * hd] * scale).astype(BF16)
        m_ref[...] = jnp.full(m_ref.shape, NEG_INF, F32)
        l_ref[...] = jnp.zeros(l_ref.shape, F32)
        acc_ref[...] = jnp.zeros(acc_ref.shape, F32)

        def body(j, carry):
            k0 = pl.multiple_of(j * tk, tk)
            kt = k_ref[0, pl.ds(k0, tk), g * hd:(g + 1) * hd].astype(BF16)
            vt = v_ref[0, pl.ds(k0, tk), g * hd:(g + 1) * hd].astype(BF16)
            st = _nt(kt, qs_ref[...])
            b0 = j * (tk // SEL_BLK)
            chosen = jnp.concatenate(
                [jnp.broadcast_to(sel_ref[0, 0, g, pl.ds(b0 + i, 1), :], (SEL_BLK, tq))
                 for i in range(tk // SEL_BLK)], axis=0)
            neg = jnp.where((chosen > 0.5) & (k0 + krow <= tt), 0.0, NEG_INF)
            st = st + jnp.concatenate([neg] * R, axis=1)
            m_old = m_ref[...]
            m_new = jnp.maximum(m_old, jnp.max(st, axis=0, keepdims=True))
            m_safe = jnp.where(m_new == NEG_INF, 0.0, m_new)
            alpha = jnp.exp(m_old - m_safe)
            p = jnp.exp(st - m_safe)
            l_ref[...] = alpha * l_ref[...] + jnp.sum(p, axis=0, keepdims=True)
            acc_ref[...] = alpha * acc_ref[...] + _tn(vt, p.astype(BF16))
            m_ref[...] = m_new
            return carry

        lax.fori_loop(0, n_tiles, body, 0)
        accn = acc_ref[...] / jnp.maximum(l_ref[...], 1e-30)
        for r in range(R):
            h = g * R + r
            o = jnp.concatenate([accn[:, r * tq:(r + 1) * tq], pad], axis=0).T[:, :hd]
            o_ref[0, :, h * hd:(h + 1) * hd] = o * _sigmoid(gate_ref[0, :, 3 * h + 1:3 * h + 2])


def _sel_attn_seq(q3, k3, v3, sel, gate3):
    b, t, _ = q3.shape
    tq = min(QBLK, t)
    tk = min(8 * QBLK, t)
    n_sel = sel.shape[-2]
    R = NSA_HEADS // NSA_KV_HEADS
    gd = NSA_KV_HEADS * HEAD_DIM
    return pl.pallas_call(
        functools.partial(_selattn_kernel, tq=tq, tk=tk, n_sel=n_sel),
        grid=(b, t // tq),
        in_specs=[pl.BlockSpec((1, tq, NSA_HEADS * HEAD_DIM), lambda i, j: (i, j, 0)),
                  pl.BlockSpec((1, t, gd), lambda i, j: (i, 0, 0)),
                  pl.BlockSpec((1, t, gd), lambda i, j: (i, 0, 0)),
                  pl.BlockSpec((1, 1, NSA_KV_HEADS, n_sel, tq), lambda i, j: (i, j, 0, 0, 0)),
                  pl.BlockSpec((1, tq, LANES), lambda i, j: (i, j, 0))],
        out_specs=pl.BlockSpec((1, tq, NSA_HEADS * HEAD_DIM), lambda i, j: (i, j, 0)),
        out_shape=jax.ShapeDtypeStruct((b, t, NSA_HEADS * HEAD_DIM), F32),
        scratch_shapes=[pltpu.VMEM((R * tq, HEAD_DIM), BF16), pltpu.VMEM((1, R * tq), F32),
                        pltpu.VMEM((1, R * tq), F32), pltpu.VMEM((HEAD_DIM, R * tq), F32)],
        compiler_params=_cparams("parallel", "parallel"), name="sel_attn_seq")(q3, k3, v3, sel, gate3)


def _selstep_kernel(*refs, n_slots, page):
    idx_ref, half_ref, skip_ref, hasnew_ref = refs[:4]
    q_ref = refs[4]
    k_refs = refs[5:5 + n_slots]
    v_refs = refs[5 + n_slots:5 + 2 * n_slots]
    nk_ref, nv_ref, gate_ref, o_ref = refs[5 + 2 * n_slots:]
    b = pl.program_id(0)
    g = pl.program_id(1)
    base = (b * NSA_KV_HEADS + g) * n_slots
    hd = HEAD_DIM
    R = NSA_HEADS // NSA_KV_HEADS
    scale = hd ** -0.5
    qg = (q_ref[0, 0] * scale).astype(BF16)
    lane_half = lax.broadcasted_iota(jnp.int32, (R, page), 1) // SEL_BLK
    nk = nk_ref[0, 0]
    nv = nv_ref[0, 0]
    s_new = _nt(qg, jnp.broadcast_to(nk, (SUBLANES, hd)).astype(BF16))[:, 0:1]
    s_new = jnp.where(hasnew_ref[b * NSA_KV_HEADS + g] == 1, s_new, NEG_INF)
    m = s_new
    scores = []
    for j in range(n_slots):
        s = _dot(qg, k_refs[j][0, 0].astype(BF16))
        s = jnp.where(lane_half == half_ref[base + j], s, NEG_INF)
        s = jnp.where(skip_ref[base + j] == 0, s, NEG_INF)
        m = jnp.maximum(m, jnp.max(s, axis=-1, keepdims=True))
        scores.append(s)
    m = jnp.where(m == NEG_INF, 0.0, m)
    e_new = jnp.exp(s_new - m)
    den = e_new
    es = []
    for s in scores:
        e = jnp.exp(s - m)
        den = den + jnp.sum(e, axis=-1, keepdims=True)
        es.append(e)
    den = jnp.maximum(den, 1e-30)
    o = (e_new / den) * nv
    for e, v_ref in zip(es, v_refs):
        o = o + _nt((e / den).astype(BF16), v_ref[0, 0].astype(BF16))
    o_ref[0, 0] = o * _sigmoid(gate_ref[0, 0, :, 1:2])


def _sel_attn_step(q4, cache_k, cache_v, page_idx, half, skip, has_new, new_k, new_v, gate4):
    b = q4.shape[0]
    n_pool, page, G, hd = cache_k.shape
    R = NSA_HEADS // NSA_KV_HEADS
    ck = _pages_gd_tok(cache_k).reshape(n_pool, G, hd, page)
    cv = _pages_gd_tok(cache_v).reshape(n_pool, G, hd, page)
    n_slots = N_SEL

    def slot_spec(j):
        return pl.BlockSpec((1, 1, hd, page),
                            lambda i, g, idx, hf, sk, hn: (idx[(i * G + g) * n_slots + j], g, 0, 0))

    grp = lambda i, g, idx, hf, sk, hn: (i, g, 0, 0)
    in_specs = ([pl.BlockSpec((1, 1, R, hd), grp)] + [slot_spec(j) for j in range(n_slots)] * 2
                + [pl.BlockSpec((1, 1, 1, hd), grp)] * 2 + [pl.BlockSpec((1, 1, R, 3), grp)])
    gs = pltpu.PrefetchScalarGridSpec(
        num_scalar_prefetch=4, grid=(b, G), in_specs=in_specs,
        out_specs=pl.BlockSpec((1, 1, R, hd), grp))
    return pl.pallas_call(
        functools.partial(_selstep_kernel, n_slots=n_slots, page=page), grid_spec=gs,
        out_shape=jax.ShapeDtypeStruct((b, G, R, hd), F32),
        compiler_params=_cparams("arbitrary", "arbitrary"), name="sel_attn_step")(
            page_idx, half, skip, has_new, q4, *([ck] * n_slots), *([cv] * n_slots), new_k, new_v, gate4)


def _mem_kernel(q_ref, k_ref, v_ref, g_ref, o_ref, *, split_heads):
    hd = MEM_HEAD_DIM
    for h in range(MEM_HEADS):
        sl = slice(h * hd, (h + 1) * hd)
        kh = k_ref[0, 0, :, h, :] if split_heads else k_ref[0, :, sl]
        vh = v_ref[0, 0, :, h, :] if split_heads else v_ref[0, :, sl]
        qh = q_ref[0, :, sl]
        qn = qh * lax.rsqrt(jnp.mean(qh * qh, axis=-1, keepdims=True) + EPS) * g_ref[...]
        s = _nt(qn.astype(BF16), kh.astype(BF16)) * (hd ** -0.5)
        m = jnp.max(s, axis=-1, keepdims=True)
        e = jnp.exp(s - m)
        p = e / jnp.sum(e, axis=-1, keepdims=True)
        o_ref[0, :, sl] = _dot(p.astype(BF16), vh.astype(BF16))


def _mem_attn(q3, k, v, gq, layer=None):
    b, t, wd = q3.shape
    tm = _row_tile(t, 512)
    split = layer is not None
    if split:
        mt = k.shape[2]
        kv_spec = pl.BlockSpec((1, 1, mt, MEM_HEADS, MEM_HEAD_DIM), lambda i, j: (layer, i, 0, 0, 0))
    else:
        mt = k.shape[1]
        kv_spec = pl.BlockSpec((1, mt, wd), lambda i, j: (i, 0, 0))
    return pl.pallas_call(
        functools.partial(_mem_kernel, split_heads=split), grid=(b, t // tm),
        in_specs=[pl.BlockSpec((1, tm, wd), lambda i, j: (i, j, 0)), kv_spec, kv_spec,
                  pl.BlockSpec((1, MEM_HEAD_DIM), lambda i, j: (0, 0))],
        out_specs=pl.BlockSpec((1, tm, wd), lambda i, j: (i, j, 0)),
        out_shape=jax.ShapeDtypeStruct((b, t, wd), F32),
        compiler_params=_cparams("parallel", "parallel"), name="mem_attn")(
            q3, k, v, gq.reshape(1, MEM_HEAD_DIM))


FFN_COLS = 256


def _ffn_kernel(*refs, ff, state_mode):
    if state_mode:
        u_ref, p2_ref, p1_ref, cw_ref, cb_ref, wo_ref, x_ref, o_ref, acc_ref = refs
    else:
        u_ref, halo_ref, cw_ref, cb_ref, wo_ref, x_ref, o_ref, acc_ref = refs
        first = pl.program_id(1) == 0

    def conv(off):
        sl = slice(off, off + FFN_COLS)
        if state_mode:
            cur, s1, s2 = u_ref[:, sl], p1_ref[:, sl], p2_ref[:, sl]
        else:
            cur = u_ref[0, :, sl]
            halo = jnp.where(first, 0.0, halo_ref[0, :, sl])
            s1 = _shift_rows(cur, halo, 1)
            s2 = _shift_rows(cur, halo, 2)
        return s2 * cw_ref[0:1, sl] + s1 * cw_ref[1:2, sl] + cur * cw_ref[2:3, sl] + cb_ref[:, sl]

    acc_ref[...] = x_ref[...] if state_mode else x_ref[0]
    for c in range(ff // FFN_COLS):
        a = conv(c * FFN_COLS)
        b = conv(ff + c * FFN_COLS)
        act = (_silu(a) * b).astype(BF16)
        acc_ref[...] += _dot(act, wo_ref[c * FFN_COLS:(c + 1) * FFN_COLS, :])
    if state_mode:
        o_ref[...] = acc_ref[...]
    else:
        o_ref[0] = acc_ref[...]


def _ffn_tail_seq(u3, cw8, cb, wo, x3):
    b, t, f2 = u3.shape
    ff = f2 // 2
    d = x3.shape[-1]
    assert ff % FFN_COLS == 0
    tm = _row_tile(t, 256)
    hb = tm // SUBLANES
    return pl.pallas_call(
        functools.partial(_ffn_kernel, ff=ff, state_mode=False),
        grid=(b, t // tm),
        in_specs=[pl.BlockSpec((1, tm, f2), lambda i, j: (i, j, 0)),
                  pl.BlockSpec((1, SUBLANES, f2), lambda i, j: (i, jnp.maximum(j * hb - 1, 0), 0)),
                  pl.BlockSpec((SUBLANES, f2), lambda i, j: (0, 0)),
                  pl.BlockSpec((1, f2), lambda i, j: (0, 0)),
                  pl.BlockSpec((ff, d), lambda i, j: (0, 0)),
                  pl.BlockSpec((1, tm, d), lambda i, j: (i, j, 0))],
        out_specs=pl.BlockSpec((1, tm, d), lambda i, j: (i, j, 0)),
        out_shape=jax.ShapeDtypeStruct((b, t, d), F32),
        scratch_shapes=[pltpu.VMEM((tm, d), F32)],
        compiler_params=_cparams("parallel", "arbitrary"), name="ffn_tail_seq")(u3, u3, cw8, cb, wo, x3)


def _ffn_tail_step(u, prev2, prev1, cw8, cb, wo, x):
    m, f2 = u.shape
    ff = f2 // 2
    d = x.shape[-1]
    tm = _row_tile(m, 128)
    row = pl.BlockSpec((tm, f2), lambda i: (i, 0))
    return pl.pallas_call(
        functools.partial(_ffn_kernel, ff=ff, state_mode=True),
        grid=(m // tm,),
        in_specs=[row, row, row,
                  pl.BlockSpec((SUBLANES, f2), lambda i: (0, 0)), pl.BlockSpec((1, f2), lambda i: (0, 0)),
                  pl.BlockSpec((ff, d), lambda i: (0, 0)), pl.BlockSpec((tm, d), lambda i: (i, 0))],
        out_specs=pl.BlockSpec((tm, d), lambda i: (i, 0)),
        out_shape=jax.ShapeDtypeStruct((m, d), F32),
        scratch_shapes=[pltpu.VMEM((tm, d), F32)],
        compiler_params=_cparams("parallel"), name="ffn_tail_step")(u, prev2, prev1, cw8, cb, wo, x)


def _pad_rows(a, rows):
    return jnp.pad(a.astype(F32), ((0, rows - a.shape[0]), (0, 0)))


def _pad_cols(w, cols):
    return jnp.pad(w, ((0, 0), (0, cols - w.shape[1])))


def _mem_block(x, bsz, t, mem_k, mem_v, P, layer, cache_layer=None):
    qm = _proj([x], P['mem_w_q'][layer], gain=P['mem_norm_x'][layer])
    om = _mem_attn(qm.reshape(bsz, t, -1), mem_k, mem_v, P['mem_q_norm'][layer], cache_layer)
    return _proj([om.reshape(bsz * t, -1)], P['mem_w_o'][layer], res=x)


def _ffn_in(x, P, layer):
    f2 = P['ffn_w_in'][layer].shape[1]
    tn = f2 // 4 if (f2 // 4) % LANES == 0 else f2
    return _proj([x], P['ffn_w_in'][layer], gain=P['ffn_norm'][layer], tn=tn)


def _dn_params(P):
    pa = jnp.zeros((1, LANES), F32).at[0, DN_HEADS:2 * DN_HEADS].set(P['l0_dn_a_log'].astype(F32))
    pd = jnp.zeros((1, LANES), F32).at[0, DN_HEADS:2 * DN_HEADS].set(P['l0_dn_dt_bias'].astype(F32))
    return _pad_rows(P['l0_dn_conv_w'], SUBLANES), pa, pd, P['l0_dn_norm'].astype(F32).reshape(1, DN_DK)


L0_SEG = (1536, 512, LANES, 512, 128, 128)
L1_SEG = (1024, 256, 256, 256, 256, 256, 256, LANES)


def _prompt_group(x_prompt, mem_prompt, P):
    bsz, t, d = x_prompt.shape
    m = bsz * t
    x = x_prompt.reshape(m, d)
    tabs = _rope_tables(jnp.arange(t, dtype=jnp.int32))

    mt = mem_prompt.shape[1]
    mem_rows = mem_prompt.reshape(bsz * mt, d)
    mem_k, mem_v = [], []
    for layer in range(2):
        kraw, v = _proj([mem_rows], P['mem_w_kv'][layer], gain=P['mem_norm_kv'][layer],
                        seg=(MEM_HEADS * MEM_HEAD_DIM,) * 2)
        mem_k.append(_headnorm(kraw, P['mem_k_norm'][layer], MEM_HEAD_DIM).reshape(bsz, mt, -1))
        mem_v.append(v.reshape(bsz, mt, -1))

    qkv, z, ba, q_b, k_b, v_b = _proj([x], P['l0_w_in'], gain=P['mix_norm'][0], seg=L0_SEG)
    cw8, pa, pd, gn = _dn_params(P)
    qkv3 = qkv.reshape(bsz, t, -1)
    w, u, qg, kd, qk, gc = _dn_prepare(qkv3, ba.reshape(bsz, t, LANES), cw8, pa, pd)
    o_dn, s_fin = _dn_recur(w, u, qg, kd, qk, gc, z.reshape(bsz, t, -1), gn)
    qs = _headnorm(q_b, P['l0_swa_q_norm'], HEAD_DIM, tabs=tabs, seq_len=t, want_norm=False)
    ks = _headnorm(k_b, P['l0_swa_k_norm'], HEAD_DIM, tabs=tabs, seq_len=t, want_norm=False)
    ks3 = ks.reshape(bsz, t, -1)
    vs3 = v_b.reshape(bsz, t, -1)
    o_sw = _band_attn(qs.reshape(bsz, t, -1), ks3, vs3, H=SWA_HEADS, G=SWA_KV_HEADS, window=SWA_WINDOW,
                      sinks=P['l0_swa_sinks'])
    mix = jnp.concatenate([o_dn, o_sw], axis=-1).reshape(m, -1)
    x = _proj([mix], P['l0_w_out'], res=x)
    x = _mem_block(x, bsz, t, mem_k[0], mem_v[0], P, 0)
    u0 = _ffn_in(x, P, 0)
    u03 = u0.reshape(bsz, t, -1)
    x = _ffn_tail_seq(u03, P['ffn_cw8'][0], P['ffn_cb'][0], P['ffn_w_out'][0], x.reshape(bsz, t, d)).reshape(m, d)
    nb = min(SWA_WINDOW, t)
    ab_state = (s_fin, qkv3[:, t - 3:], ks3[:, t - nb:].reshape(bsz, nb, SWA_KV_HEADS, HEAD_DIM),
                vs3[:, t - nb:].reshape(bsz, nb, SWA_KV_HEADS, HEAD_DIM))

    q, kc, vc, ksr, vsel, kwr, vw, gr = _proj([x], P['l1_w_in'], gain=P['mix_norm'][1], seg=L1_SEG)
    qn, qr = _headnorm(q, P['l1_q_norm'], HEAD_DIM, tabs=tabs, seq_len=t)
    ksel = _headnorm(ksr, P['l1_k_norm_sel'], HEAD_DIM, tabs=tabs, seq_len=t, want_norm=False)
    kw = _headnorm(kwr, P['l1_k_norm_win'], HEAD_DIM, tabs=tabs, seq_len=t, want_norm=False)
    gd = NSA_KV_HEADS * HEAD_DIM
    r3 = lambda a: a.reshape(bsz, t, -1)
    kcmp = _compress_seq(r3(kc), P['l1_cmp_pe_k'], P['l1_cmp_w1_k'], P['l1_cmp_w2_k'], P['l1_k_norm_cmp'])
    vcmp = _compress_seq(r3(vc), P['l1_cmp_pe_v'], P['l1_cmp_w1_v'], P['l1_cmp_w2_v'])
    gr3 = r3(gr)
    o_c, sel = _cmp_attn_seq(r3(qn), kcmp, vcmp, gr3)
    o_s = _sel_attn_seq(r3(qr), r3(ksel), r3(vsel), sel, gr3)
    o_w = _band_attn(r3(qr), r3(kw), r3(vw), H=NSA_HEADS, G=NSA_KV_HEADS, window=NSA_WINDOW,
                     gate3=gr3, gate_col=2)
    x = _proj([o_c.reshape(m, -1), o_s.reshape(m, -1), o_w.reshape(m, -1)], P['l1_w_out'], res=x)
    x = _mem_block(x, bsz, t, mem_k[1], mem_v[1], P, 1)
    u1 = _ffn_in(x, P, 1)
    u13 = u1.reshape(bsz, t, -1)
    x = _ffn_tail_seq(u13, P['ffn_cw8'][1], P['ffn_cb'][1], P['ffn_w_out'][1], x.reshape(bsz, t, d)).reshape(m, d)
    nw = min(NSA_WINDOW, t)
    r4 = lambda a: a.reshape(bsz, t, NSA_KV_HEADS, HEAD_DIM)
    nsa_state = (r4(kc), r4(vc), r4(ksel), r4(vsel), r4(kw)[:, t - nw:], r4(vw)[:, t - nw:])
    mem_kr = jnp.stack([k.reshape(bsz, mt, MEM_HEADS, MEM_HEAD_DIM) for k in mem_k])
    mem_vr = jnp.stack([v.reshape(bsz, mt, MEM_HEADS, MEM_HEAD_DIM) for v in mem_v])
    ffn_state = jnp.stack([u03[:, t - 2:], u13[:, t - 2:]])
    return x.reshape(bsz, t, d), ab_state, nsa_state, mem_kr, mem_vr, ffn_state


def _sample_group(x_sample, state_dn, state_dn_conv, cache_swa_k, cache_swa_v, cache_cmp_k, cache_cmp_v,
                  cache_sel_k, cache_sel_v, cache_win_k, cache_win_v, cache_mem_k, cache_mem_v,
                  state_ffn_conv, page_table, P):
    bsz, t, d = x_sample.shape
    assert t == 1
    page = cache_cmp_k.shape[1]
    n_pages = page_table.shape[1]
    past = n_pages * page
    assert cache_swa_k.shape[1] == SWA_WINDOW and cache_win_k.shape[1] == NSA_WINDOW
    x = x_sample.reshape(bsz, d)
    tabs = _rope_tables(jnp.full((1,), past, jnp.int32))

    qkv, z, ba, q_b, k_b, v_b = _proj([x], P['l0_w_in'], gain=P['mix_norm'][0], seg=L0_SEG)
    cw8, pa, pd, gn = _dn_params(P)
    o_dn, s_new, conv_new = _dn_step(qkv, state_dn_conv, ba, z, state_dn, cw8, pa, pd, gn)
    qs = _headnorm(q_b, P['l0_swa_q_norm'], HEAD_DIM, tabs=tabs, want_norm=False)
    ks = _headnorm(k_b, P['l0_swa_k_norm'], HEAD_DIM, tabs=tabs, want_norm=False)
    swa_k, swa_v, o_sw = _step_window_attn(
        qs.reshape(bsz, SWA_HEADS, HEAD_DIM), cache_swa_k.reshape(bsz, SWA_WINDOW, -1),
        cache_swa_v.reshape(bsz, SWA_WINDOW, -1), ks, v_b, H=SWA_HEADS, G=SWA_KV_HEADS,
        sinks=P['l0_swa_sinks'])
    mix = jnp.concatenate([o_dn.reshape(bsz, -1), o_sw.reshape(bsz, -1)], axis=-1)
    x = _proj([mix], P['l0_w_out'], res=x)
    x = _mem_block(x, bsz, 1, cache_mem_k, cache_mem_v, P, 0, cache_layer=0)
    u0 = _ffn_in(x, P, 0)
    x = _ffn_tail_step(u0, state_ffn_conv[0, :, 0], state_ffn_conv[0, :, 1], P['ffn_cw8'][0], P['ffn_cb'][0],
                       P['ffn_w_out'][0], x)
    ab_state = (s_new, conv_new, swa_k.reshape(bsz, SWA_WINDOW, SWA_KV_HEADS, HEAD_DIM),
                swa_v.reshape(bsz, SWA_WINDOW, SWA_KV_HEADS, HEAD_DIM))

    q, kc, vc, ksr, vsel, kwr, vw, gr = _proj([x], P['l1_w_in'], gain=P['mix_norm'][1], seg=L1_SEG)
    qn, qr = _headnorm(q, P['l1_q_norm'], HEAD_DIM, tabs=tabs)
    ksel = _headnorm(ksr, P['l1_k_norm_sel'], HEAD_DIM, tabs=tabs, want_norm=False)
    kw = _headnorm(kwr, P['l1_k_norm_win'], HEAD_DIM, tabs=tabs, want_norm=False)
    G, R = NSA_KV_HEADS, NSA_HEADS // NSA_KV_HEADS
    total = past + 1
    n_cmp = (total - CMP_BLK) // CMP_STRIDE + 1
    n_sel = -(-total // SEL_BLK)
    ns_pad = -(-n_sel // LANES) * LANES
    kcmp = _compress_paged(cache_cmp_k, page_table, P['l1_cmp_pe_k'], P['l1_cmp_w1_k'], P['l1_cmp_w2_k'],
                           P['l1_k_norm_cmp'])
    vcmp = _compress_paged(cache_cmp_v, page_table, P['l1_cmp_pe_v'], P['l1_cmp_w1_v'], P['l1_cmp_w2_v'])
    gate_h = gr[:, :3 * NSA_HEADS].reshape(bsz, NSA_HEADS, 3)
    o_c, imp = _cmp_attn_step(qn.reshape(bsz, NSA_HEADS, HEAD_DIM), kcmp, vcmp, gate_h, pos=past, n_cmp=n_cmp,
                              n_sel=n_sel, ns_pad=ns_pad)
    imp_t = imp.reshape(bsz * G, ns_pad).T
    rank = _rank_call(imp_t, n_sel)
    rank = jnp.where(jnp.arange(ns_pad)[:, None] < n_sel, rank, 1e9)
    blocks = jnp.argsort(rank, axis=0)[:N_SEL].T.astype(jnp.int32)
    blocks = blocks.reshape(bsz, G, N_SEL)
    per_page = page // SEL_BLK
    is_new = blocks >= n_pages * per_page
    blk_c = jnp.minimum(blocks, n_pages * per_page - 1)
    page_idx = jnp.take_along_axis(page_table[:, None, :], blk_c // per_page, axis=2)
    o_s = _sel_attn_step(
        qr.reshape(bsz, G, R, HEAD_DIM), cache_sel_k, cache_sel_v, page_idx.reshape(-1).astype(jnp.int32),
        (blk_c % per_page).reshape(-1).astype(jnp.int32), is_new.reshape(-1).astype(jnp.int32),
        jnp.any(is_new, axis=-1).reshape(-1).astype(jnp.int32), ksel.reshape(bsz, G, 1, HEAD_DIM),
        vsel.reshape(bsz, G, 1, HEAD_DIM), gate_h.reshape(bsz, G, R, 3))
    win_k, win_v, o_w = _step_window_attn(
        qr.reshape(bsz, NSA_HEADS, HEAD_DIM), cache_win_k.reshape(bsz, NSA_WINDOW, -1),
        cache_win_v.reshape(bsz, NSA_WINDOW, -1), kw, vw, H=NSA_HEADS, G=NSA_KV_HEADS,
        gate=gate_h, gate_col=2)
    x = _proj([o_c.reshape(bsz, -1), o_s.reshape(bsz, -1), o_w.reshape(bsz, -1)], P['l1_w_out'], res=x)
    x = _mem_block(x, bsz, 1, cache_mem_k, cache_mem_v, P, 1, cache_layer=1)
    u1 = _ffn_in(x, P, 1)
    x = _ffn_tail_step(u1, state_ffn_conv[1, :, 0], state_ffn_conv[1, :, 1], P['ffn_cw8'][1], P['ffn_cb'][1],
                       P['ffn_w_out'][1], x)
    r4 = lambda a: a.reshape(bsz, 1, G, HEAD_DIM)
    nsa_state = (r4(kc), r4(vc), r4(ksel), r4(vsel), win_k.reshape(bsz, NSA_WINDOW, G, HEAD_DIM),
                 win_v.reshape(bsz, NSA_WINDOW, G, HEAD_DIM))
    ffn_state = jnp.stack([jnp.stack([state_ffn_conv[0, :, 1], u0], axis=1),
                           jnp.stack([state_ffn_conv[1, :, 1], u1], axis=1)])
    return x.reshape(bsz, 1, d), ab_state, nsa_state, ffn_state


def kernel(x_prompt, x_sample, state_dn, state_dn_conv, cache_swa_k, cache_swa_v, cache_cmp_k, cache_cmp_v, cache_sel_k, cache_sel_v, cache_win_k, cache_win_v, cache_mem_k, cache_mem_v, state_ffn_conv, page_table, mem_prompt, mix_norm, l0_w_in, l0_dn_conv_w, l0_dn_a_log, l0_dn_dt_bias, l0_dn_norm, l0_swa_q_norm, l0_swa_k_norm, l0_swa_sinks, l0_w_out, l1_w_in, l1_q_norm, l1_k_norm_cmp, l1_k_norm_sel, l1_k_norm_win, l1_cmp_pe_k, l1_cmp_w1_k, l1_cmp_w2_k, l1_cmp_pe_v, l1_cmp_w1_v, l1_cmp_w2_v, l1_w_out, mem_norm_x, mem_norm_kv, mem_w_q, mem_w_k, mem_w_v, mem_q_norm, mem_k_norm, mem_w_o, ffn_norm, ffn_w_in, ffn_conv_w, ffn_conv_b, ffn_w_out):
    n_b = 2 * DN_HEADS
    c0 = 1536 + 512
    w0 = jnp.concatenate([l0_w_in[:, :c0], _pad_cols(l0_w_in[:, c0:c0 + n_b], LANES), l0_w_in[:, c0 + n_b:]], axis=1)
    c1 = (NSA_HEADS + 6 * NSA_KV_HEADS) * HEAD_DIM
    w1 = jnp.concatenate([l1_w_in[:, :c1], _pad_cols(l1_w_in[:, c1:], LANES)], axis=1)
    P = dict(
        mix_norm=mix_norm, l0_w_in=w0.astype(BF16), l0_dn_conv_w=l0_dn_conv_w, l0_dn_a_log=l0_dn_a_log,
        l0_dn_dt_bias=l0_dn_dt_bias, l0_dn_norm=l0_dn_norm, l0_swa_q_norm=l0_swa_q_norm,
        l0_swa_k_norm=l0_swa_k_norm, l0_swa_sinks=l0_swa_sinks, l0_w_out=l0_w_out.astype(BF16),
        l1_w_in=w1.astype(BF16), l1_q_norm=l1_q_norm, l1_k_norm_cmp=l1_k_norm_cmp, l1_k_norm_sel=l1_k_norm_sel,
        l1_k_norm_win=l1_k_norm_win, l1_cmp_pe_k=l1_cmp_pe_k, l1_cmp_w1_k=l1_cmp_w1_k, l1_cmp_w2_k=l1_cmp_w2_k,
        l1_cmp_pe_v=l1_cmp_pe_v, l1_cmp_w1_v=l1_cmp_w1_v, l1_cmp_w2_v=l1_cmp_w2_v,
        l1_w_out=l1_w_out.astype(BF16), mem_norm_x=mem_norm_x, mem_norm_kv=mem_norm_kv,
        mem_w_q=mem_w_q.astype(BF16), mem_w_kv=jnp.concatenate([mem_w_k, mem_w_v], axis=-1).astype(BF16),
        mem_q_norm=mem_q_norm, mem_k_norm=mem_k_norm, mem_w_o=mem_w_o.astype(BF16), ffn_norm=ffn_norm,
        ffn_w_in=ffn_w_in.astype(BF16),
        ffn_cw8=jnp.pad(ffn_conv_w.astype(F32), ((0, 0), (0, SUBLANES - ffn_conv_w.shape[1]), (0, 0))),
        ffn_cb=ffn_conv_b.astype(F32)[:, None, :], ffn_w_out=ffn_w_out.astype(BF16))

    y_p, ab_p, nsa_p, p_mem_k, p_mem_v, p_ffn = _prompt_group(x_prompt, mem_prompt, P)
    y_s, ab_s, nsa_s, s_ffn = _sample_group(
        x_sample, state_dn, state_dn_conv, cache_swa_k, cache_swa_v, cache_cmp_k, cache_cmp_v, cache_sel_k,
        cache_sel_v, cache_win_k, cache_win_v, cache_mem_k, cache_mem_v, state_ffn_conv, page_table, P)
    return (y_p, y_s, *ab_p, *nsa_p, p_mem_k, p_mem_v, p_ffn, *ab_s, *nsa_s, s_ffn)
```

```python
import functools

import numpy as np
import jax
import jax.numpy as jnp
from jax import lax
from jax.experimental import pallas as pl
from jax.experimental.pallas import tpu as pltpu

F32 = jnp.float32
BF16 = jnp.bfloat16
HI = lax.Precision.HIGHEST
NEG_INF = float("-inf")

EPS = 1e-6
HEAD_DIM = 64
ROT_DIM = HEAD_DIM // 4
ROPE_THETA = 500000.0
QBLK = 128
DN_HEADS = 4
DN_DK = 128
DN_CHUNK = 64
SWA_HEADS = 8
SWA_KV_HEADS = 2
SWA_WINDOW = 128
NSA_HEADS = 16
NSA_KV_HEADS = 4
CMP_BLK = 32
CMP_STRIDE = 16
SEL_BLK = 64
N_SEL = 16
NSA_WINDOW = 512
SEL_BONUS = 1e4
MEM_HEADS = 4
MEM_HEAD_DIM = 128

LANES = 128
SUBLANES = 8
VMEM_LIMIT = 56 << 20
PAGES_PER_STEP = 64


def _cparams(*sem):
    return pltpu.CompilerParams(dimension_semantics=sem, vmem_limit_bytes=VMEM_LIMIT)


def _nt(a, b, precision=None):
    return lax.dot_general(a, b, (((1,), (1,)), ((), ())), precision=precision,
                           preferred_element_type=F32)


def _tn(a, b, precision=None):
    return lax.dot_general(a, b, (((0,), (0,)), ((), ())), precision=precision,
                           preferred_element_type=F32)


def _dot(a, b, precision=None):
    return jnp.dot(a, b, precision=precision, preferred_element_type=F32)


def _hi_lo(a):
    hi = a.astype(BF16)
    return hi, (a - hi.astype(F32)).astype(BF16)


def _dot3(a, b):
    ah, al = _hi_lo(a)
    bh, bl = _hi_lo(b)
    return _dot(ah, bh) + _dot(ah, bl) + _dot(al, bh)


def _nt3(a, b):
    ah, al = _hi_lo(a)
    bh, bl = _hi_lo(b)
    return _nt(ah, bh) + _nt(ah, bl) + _nt(al, bh)


def _tn3(a, b):
    ah, al = _hi_lo(a)
    bh, bl = _hi_lo(b)
    return _tn(ah, bh) + _tn(ah, bl) + _tn(al, bh)


def _silu(x):
    return x / (1.0 + jnp.exp(-x))


def _sigmoid(x):
    return 1.0 / (1.0 + jnp.exp(-x))


def _softplus(x):
    return jnp.maximum(x, 0.0) + jnp.log(1.0 + jnp.exp(-jnp.abs(x)))


def _split3(x):
    a = x.astype(BF16)
    r = x - a.astype(F32)
    b = r.astype(BF16)
    c = (r - b.astype(F32)).astype(BF16)
    return a, b, c


def _row_tile(m, pref):
    for t in (1024, 512, 256, 128, 64, 32, 16, 8):
        if t <= pref and m % t == 0:
            return t
    return m


def _shift_rows(cur, halo, k):
    r = pltpu.roll(cur, k, 0)
    h = pltpu.roll(halo, k, 0)
    row = lax.broadcasted_iota(jnp.int32, h.shape, 0)
    top = jnp.where(row < k, h, r[0:SUBLANES])
    if cur.shape[0] == SUBLANES:
        return top
    return jnp.concatenate([top, r[SUBLANES:]], axis=0)


def _proj_kernel(*refs, n_add, has_gain, has_res, seg):
    it = iter(refs)
    adds = [next(it) for _ in range(n_add)]
    gain = next(it) if has_gain else None
    w = next(it)
    res = next(it) if has_res else None
    outs = [next(it) for _ in seg]
    xn = next(it)

    @pl.when(pl.program_id(1) == 0)
    def _():
        x = adds[0][...]
        for a in adds[1:]:
            x = x + a[...]
        if has_gain:
            ms = jnp.mean(x * x, axis=-1, keepdims=True)
            x = x * lax.rsqrt(ms + EPS) * gain[...]
        xn[...] = x.astype(BF16)

    acc = _dot(xn[...], w[...])
    if has_res:
        acc = acc + res[...]
    if len(seg) == 1:
        outs[0][...] = acc
    else:
        off = 0
        for o, s in zip(outs, seg):
            o[...] = acc[:, off:off + s]
            off += s


def _proj(adds, w, *, gain=None, res=None, seg=None, tn=None, tm_pref=512):
    m, k = adds[0].shape
    n = w.shape[1]
    tm = _row_tile(m, tm_pref)
    tn = n if tn is None else tn
    seg = (n,) if seg is None else tuple(seg)
    assert n % tn == 0 and (len(seg) == 1 or tn == n) and sum(seg) == n
    in_specs = [pl.BlockSpec((tm, k), lambda i, j: (i, 0)) for _ in adds]
    args = list(adds)
    if gain is not None:
        in_specs.append(pl.BlockSpec((1, k), lambda i, j: (0, 0)))
        args.append(gain.reshape(1, k))
    in_specs.append(pl.BlockSpec((k, tn), lambda i, j: (0, j)))
    args.append(w)
    if res is not None:
        in_specs.append(pl.BlockSpec((tm, tn), lambda i, j: (i, j)))
        args.append(res)
    if len(seg) == 1:
        out_shape = [jax.ShapeDtypeStruct((m, n), F32)]
        out_specs = [pl.BlockSpec((tm, tn), lambda i, j: (i, j))]
    else:
        out_shape = [jax.ShapeDtypeStruct((m, s), F32) for s in seg]
        out_specs = [pl.BlockSpec((tm, s), lambda i, j: (i, 0)) for s in seg]
    outs = pl.pallas_call(
        functools.partial(_proj_kernel, n_add=len(adds), has_gain=gain is not None,
                          has_res=res is not None, seg=seg),
        grid=(m // tm, n // tn), in_specs=in_specs, out_specs=out_specs, out_shape=out_shape,
        scratch_shapes=[pltpu.VMEM((tm, k), BF16)],
        compiler_params=_cparams("parallel", "arbitrary"), name="proj")(*args)
    return outs[0] if len(seg) == 1 else outs


def _hn_kernel(*refs, hd, want_norm, want_rope):
    x_ref, g_ref = refs[0], refs[1]
    i = 2
    if want_rope:
        c_ref, sa_ref, sb_ref = refs[i:i + 3]
        i += 3
    outs = refs[i:]
    width = x_ref.shape[1]
    r = lax.broadcasted_iota(jnp.int32, (LANES, LANES), 0) // hd
    c = lax.broadcasted_iota(jnp.int32, (LANES, LANES), 1) // hd
    bd = jnp.where(r == c, 1.0, 0.0).astype(BF16)
    for cb in range(width // LANES):
        sl = slice(cb * LANES, (cb + 1) * LANES)
        xc = x_ref[:, sl]
        x2 = xc * xc
        hi = x2.astype(BF16)
        lo = (x2 - hi.astype(F32)).astype(BF16)
        ss = _dot(hi, bd) + _dot(lo, bd)
        xc = xc * lax.rsqrt(ss * (1.0 / hd) + EPS) * g_ref[:, sl]
        k = 0
        if want_norm:
            outs[k][:, sl] = xc
            k += 1
        if want_rope:
            outs[k][:, sl] = (xc * c_ref[...] + pltpu.roll(xc, LANES - ROT_DIM // 2, 1) * sa_ref[...]
                              + pltpu.roll(xc, ROT_DIM // 2, 1) * sb_ref[...])


def _rope_tables(pos):
    half = ROT_DIM // 2
    inv = ROPE_THETA ** (-jnp.arange(half, dtype=F32) / half)
    ang = pos.astype(F32)[:, None] * inv[None, :]
    jj = np.arange(LANES) % HEAD_DIM
    cos = jnp.cos(ang)[:, jj % half]
    sin = jnp.sin(ang)[:, jj % half]
    c = jnp.where(jj[None, :] < ROT_DIM, cos, 1.0)
    sa = jnp.where(jj[None, :] < half, -sin, 0.0)
    sb = jnp.where((jj[None, :] >= half) & (jj[None, :] < ROT_DIM), sin, 0.0)
    return c, sa, sb


def _headnorm(x, gain, hd, *, tabs=None, seq_len=None, want_norm=True):
    m, width = x.shape
    want_rope = tabs is not None
    tm = _row_tile(seq_len if (want_rope and tabs[0].shape[0] > 1) else m, 512)
    g = jnp.tile(gain.reshape(1, hd), (1, width // hd))
    in_specs = [pl.BlockSpec((tm, width), lambda i: (i, 0)), pl.BlockSpec((1, width), lambda i: (0, 0))]
    args = [x, g]
    if want_rope:
        if tabs[0].shape[0] > 1:
            nt = seq_len // tm
            tspec = pl.BlockSpec((tm, LANES), lambda i: (i % nt, 0))
        else:
            tspec = pl.BlockSpec((1, LANES), lambda i: (0, 0))
        in_specs += [tspec] * 3
        args += list(tabs)
    n_out = int(want_norm) + int(want_rope)
    outs = pl.pallas_call(
        functools.partial(_hn_kernel, hd=hd, want_norm=want_norm, want_rope=want_rope),
        grid=(m // tm,), in_specs=in_specs,
        out_specs=[pl.BlockSpec((tm, width), lambda i: (i, 0))] * n_out,
        out_shape=[jax.ShapeDtypeStruct((m, width), F32)] * n_out,
        compiler_params=_cparams("parallel"), name="headnorm")(*args)
    return outs[0] if n_out == 1 else outs


def _dn_gates(ba, pa_ref, pd_ref):
    beta = _sigmoid(ba)
    g = -jnp.exp(pa_ref[...]) * _softplus(ba + pd_ref[...])
    return beta, g


def _dna_kernel(qkv_ref, halo_ref, ba_ref, cw_ref, pa_ref, pd_ref,
                w_o, u_o, qg_o, kd_o, qk_o, gc_o, *, C):
    n = pl.program_id(1)
    x = qkv_ref[0]
    halo = jnp.where(n == 0, 0.0, halo_ref[0])
    c = (x * cw_ref[3:4, :] + _shift_rows(x, halo, 1) * cw_ref[2:3, :]
         + _shift_rows(x, halo, 2) * cw_ref[1:2, :] + _shift_rows(x, halo, 3) * cw_ref[0:1, :])
    c = _silu(c)
    beta_all, gmat = _dn_gates(ba_ref[0], pa_ref, pd_ref)
    H, dk = DN_HEADS, DN_DK
    HC = H * C
    ri = lax.broadcasted_iota(jnp.int32, (C, C), 0)
    ci = lax.broadcasted_iota(jnp.int32, (C, C), 1)
    gc = _dot(jnp.where(ri >= ci, 1.0, 0.0), gmat, HI)
    gc_o[0] = gc
    qs, ks, vs, bs, gs = [], [], [], [], []
    for h in range(H):
        qh = c[:, h * dk:(h + 1) * dk]
        kh = c[:, (H + h) * dk:(H + h + 1) * dk]
        qs.append(qh * lax.rsqrt(jnp.sum(qh * qh, axis=-1, keepdims=True) + EPS) * (dk ** -0.5))
        ks.append(kh * lax.rsqrt(jnp.sum(kh * kh, axis=-1, keepdims=True) + EPS))
        vs.append(c[:, (2 * H + h) * dk:(2 * H + h + 1) * dk])
        bs.append(beta_all[:, h:h + 1])
        gs.append(gc[:, H + h:H + h + 1])
    qst = jnp.concatenate(qs, axis=0)
    kst = jnp.concatenate(ks, axis=0)
    vst = jnp.concatenate(vs, axis=0)
    bst = jnp.concatenate(bs, axis=0)
    gcol = jnp.concatenate(gs, axis=0)
    lane0 = lax.broadcasted_iota(jnp.int32, (HC, LANES), 1) == 0
    e0 = jnp.where((lax.broadcasted_iota(jnp.int32, (SUBLANES, LANES), 0) == 0)
                   & (lax.broadcasted_iota(jnp.int32, (SUBLANES, LANES), 1) == 0), 1.0, 0.0)
    grow = _nt(e0, jnp.where(lane0, gcol, 0.0), HI)[0:1]
    rr = lax.broadcasted_iota(jnp.int32, (HC, HC), 0)
    cc = lax.broadcasted_iota(jnp.int32, (HC, HC), 1)
    same = (rr // C) == (cc // C)
    lower = same & (rr >= cc)
    decay = jnp.where(lower, jnp.exp(jnp.where(lower, gcol - grow, 0.0)), 0.0)
    kbst = kst * bst
    lmat = jnp.where(same & (rr > cc), _nt3(kbst, kst) * decay, 0.0)
    tinv = jnp.where(rr == cc, 1.0, 0.0) - lmat
    n_fac = max(int(np.ceil(np.log2(C))) - 1, 0)
    if n_fac > 0:
        p = _dot3(lmat, lmat)
        for f in range(n_fac):
            tinv = tinv + _dot3(tinv, p)
            if f + 1 < n_fac:
                p = _dot3(p, p)
    eg = jnp.exp(gcol)
    wu = _dot3(tinv, jnp.concatenate([kbst * eg, vst * bst], axis=1))
    qk = jnp.where(lower, _nt3(qst, kst) * decay, 0.0)
    qk_sum = qk[0:C]
    for h in range(1, H):
        qk_sum = qk_sum + qk[h * C:(h + 1) * C]
    qk_o[0] = qk_sum
    for h in range(H):
        sl = slice(h * dk, (h + 1) * dk)
        rows = slice(h * C, (h + 1) * C)
        w_o[0, :, sl] = wu[rows, :dk]
        u_o[0, :, sl] = wu[rows, dk:]
        qg_o[0, :, sl] = qs[h] * eg[rows]
        kd_o[0, :, sl] = ks[h] * jnp.exp(gs[h][C - 1:C, :] - gs[h])


def _dn_prepare(qkv3, ba3, cw8, pa, pd):
    b, t, wq = qkv3.shape
    C = DN_CHUNK
    assert t % C == 0
    nh = DN_HEADS * DN_DK
    hb = C // SUBLANES
    outs = pl.pallas_call(
        functools.partial(_dna_kernel, C=C),
        grid=(b, t // C),
        in_specs=[pl.BlockSpec((1, C, wq), lambda i, n: (i, n, 0)),
                  pl.BlockSpec((1, SUBLANES, wq), lambda i, n: (i, jnp.maximum(n * hb - 1, 0), 0)),
                  pl.BlockSpec((1, C, LANES), lambda i, n: (i, n, 0)),
                  pl.BlockSpec((SUBLANES, wq), lambda i, n: (0, 0)),
                  pl.BlockSpec((1, LANES), lambda i, n: (0, 0)),
                  pl.BlockSpec((1, LANES), lambda i, n: (0, 0))],
        out_specs=[pl.BlockSpec((1, C, nh), lambda i, n: (i, n, 0))] * 4
        + [pl.BlockSpec((1, C, DN_HEADS * C), lambda i, n: (i, n, 0)),
           pl.BlockSpec((1, C, LANES), lambda i, n: (i, n, 0))],
        out_shape=[jax.ShapeDtypeStruct((b, t, nh), F32)] * 4
        + [jax.ShapeDtypeStruct((b, t, DN_HEADS * C), F32), jax.ShapeDtypeStruct((b, t, LANES), F32)],
        compiler_params=_cparams("parallel", "parallel"), name="dn_prepare")(qkv3, qkv3, ba3, cw8, pa, pd)
    return outs


def _dnb_kernel(w_ref, u_ref, qg_ref, kd_ref, qk_ref, gc_ref, z_ref, gn_ref, o_ref, s_out, s_ref, *, C, nb):
    n = pl.program_id(0)

    @pl.when(n == 0)
    def _():
        s_ref[...] = jnp.zeros_like(s_ref)

    dk = DN_DK
    for b in range(nb):
        for h in range(DN_HEADS):
            sl = slice(h * dk, (h + 1) * dk)
            s = s_ref[b * DN_HEADS + h]
            v_new = u_ref[b, :, sl] - _dot3(w_ref[b, :, sl], s)
            o = _dot3(qg_ref[b, :, sl], s) + _dot3(qk_ref[b, :, h * C:(h + 1) * C], v_new)
            dl = jnp.exp(gc_ref[b, C - 1:C, DN_HEADS + h:DN_HEADS + h + 1])
            s_ref[b * DN_HEADS + h] = s * dl + _tn3(kd_ref[b, :, sl], v_new)
            o = o * lax.rsqrt(jnp.mean(o * o, axis=-1, keepdims=True) + EPS) * gn_ref[...]
            o_ref[b, :, sl] = o * _silu(z_ref[b, :, sl])

    @pl.when(n == pl.num_programs(0) - 1)
    def _():
        for b in range(nb):
            for h in range(DN_HEADS):
                s_out[b, h] = s_ref[b * DN_HEADS + h]


def _dn_recur(w, u, qg, kd, qk, gc, z3, gn):
    b, t, nh = w.shape
    C = DN_CHUNK
    big = pl.BlockSpec((b, C, nh), lambda n: (0, n, 0))
    return pl.pallas_call(
        functools.partial(_dnb_kernel, C=C, nb=b),
        grid=(t // C,),
        in_specs=[big, big, big, big,
                  pl.BlockSpec((b, C, DN_HEADS * C), lambda n: (0, n, 0)),
                  pl.BlockSpec((b, C, LANES), lambda n: (0, n, 0)),
                  big, pl.BlockSpec((1, DN_DK), lambda n: (0, 0))],
        out_specs=[big, pl.BlockSpec((b, DN_HEADS, DN_DK, DN_DK), lambda n: (0, 0, 0, 0))],
        out_shape=[jax.ShapeDtypeStruct((b, t, nh), F32),
                   jax.ShapeDtypeStruct((b, DN_HEADS, DN_DK, DN_DK), F32)],
        scratch_shapes=[pltpu.VMEM((b * DN_HEADS, DN_DK, DN_DK), F32)],
        compiler_params=_cparams("arbitrary"), name="dn_recur")(w, u, qg, kd, qk, gc, z3, gn)


def _dns_kernel(qkv_ref, st_ref, ba_ref, z_ref, s_ref, cw_ref, pa_ref, pd_ref, gn_ref,
                o_ref, so_ref, sto_ref):
    new = qkv_ref[0]
    st = st_ref[0]
    c = (st[0:1] * cw_ref[0:1, :] + st[1:2] * cw_ref[1:2, :] + st[2:3] * cw_ref[2:3, :]
         + new * cw_ref[3:4, :])
    c = _silu(c)
    sto_ref[0, 0:2, :] = st[1:3]
    sto_ref[0, 2:3, :] = new
    beta_all, gmat = _dn_gates(ba_ref[0], pa_ref, pd_ref)
    dk = DN_DK
    row = lax.broadcasted_iota(jnp.int32, (SUBLANES, dk), 0)
    for h in range(DN_HEADS):
        qh = c[:, h * dk:(h + 1) * dk]
        kh = c[:, (DN_HEADS + h) * dk:(DN_HEADS + h + 1) * dk]
        vh = c[:, (2 * DN_HEADS + h) * dk:(2 * DN_HEADS + h + 1) * dk]
        qh = qh * lax.rsqrt(jnp.sum(qh * qh, axis=-1, keepdims=True) + EPS) * (dk ** -0.5)
        kh = kh * lax.rsqrt(jnp.sum(kh * kh, axis=-1, keepdims=True) + EPS)
        bh = beta_all[:, h:h + 1]
        eg = jnp.exp(gmat[:, DN_HEADS + h:DN_HEADS + h + 1])
        s = s_ref[0, h]
        kq = jnp.where(row == 0, kh, jnp.where(row == 1, qh, 0.0))
        ks_qs = _dot(kq, s, HI)
        v_new = bh * (vh - eg * ks_qs[0:1])
        o = eg * ks_qs[1:2] + jnp.sum(qh * kh, axis=-1, keepdims=True) * v_new
        k8 = jnp.where(row == 0, kh, 0.0)
        v8 = jnp.where(row == 0, v_new, 0.0)
        so_ref[0, h] = s * eg + _tn(k8, v8, HI)
        o = o * lax.rsqrt(jnp.mean(o * o, axis=-1, keepdims=True) + EPS) * gn_ref[...]
        o_ref[0, :, h * dk:(h + 1) * dk] = o * _silu(z_ref[0, :, h * dk:(h + 1) * dk])


def _dn_step(qkv, conv_state, ba, z, state, cw8, pa, pd, gn):
    b, wq = qkv.shape
    nh = DN_HEADS * DN_DK
    one = lambda i: (i, 0, 0)
    cst = lambda i: (0, 0)
    return pl.pallas_call(
        _dns_kernel, grid=(b,),
        in_specs=[pl.BlockSpec((1, 1, wq), one), pl.BlockSpec((1, 3, wq), one),
                  pl.BlockSpec((1, 1, LANES), one), pl.BlockSpec((1, 1, nh), one),
                  pl.BlockSpec((1, DN_HEADS, DN_DK, DN_DK), lambda i: (i, 0, 0, 0)),
                  pl.BlockSpec((SUBLANES, wq), cst), pl.BlockSpec((1, LANES), cst),
                  pl.BlockSpec((1, LANES), cst), pl.BlockSpec((1, DN_DK), cst)],
        out_specs=[pl.BlockSpec((1, 1, nh), one),
                   pl.BlockSpec((1, DN_HEADS, DN_DK, DN_DK), lambda i: (i, 0, 0, 0)),
                   pl.BlockSpec((1, 3, wq), one)],
        out_shape=[jax.ShapeDtypeStruct((b, 1, nh), F32),
                   jax.ShapeDtypeStruct(state.shape, F32),
                   jax.ShapeDtypeStruct(conv_state.shape, F32)],
        compiler_params=_cparams("parallel"), name="dn_step")(
            qkv.reshape(b, 1, wq), conv_state, ba.reshape(b, 1, LANES), z.reshape(b, 1, nh),
            state, cw8, pa, pd, gn)


def _band_kernel(*refs, H, G, window, tq, span, T, has_sink, gate_col):
    q_ref, k_ref, v_ref = refs[:3]
    i = 3
    sink_ref = gate_ref = None
    if has_sink:
        sink_ref = refs[i]
        i += 1
    if gate_col is not None:
        gate_ref = refs[i]
        i += 1
    o_ref = refs[i]
    qi = pl.program_id(1)
    start = jnp.minimum(jnp.maximum(qi * tq - window, 0), T - span)
    start = pl.multiple_of(start, tq)
    kb = k_ref[0, pl.ds(start, span), :]
    vb = v_ref[0, pl.ds(start, span), :]
    R = H // G
    hd = HEAD_DIM
    scale = hd ** -0.5
    qpos = qi * tq + lax.broadcasted_iota(jnp.int32, (span, tq), 1)
    kpos = start + lax.broadcasted_iota(jnp.int32, (span, tq), 0)
    d = qpos - kpos
    neg = jnp.where((d >= 0) & (d < window), 0.0, NEG_INF)
    neg = jnp.concatenate([neg] * R, axis=1)
    pad = jnp.zeros((LANES - hd, tq), F32)
    for g in range(G):
        kg = kb[:, g * hd:(g + 1) * hd].astype(BF16)
        vg = vb[:, g * hd:(g + 1) * hd].astype(BF16)
        qs = jnp.concatenate([(q_ref[0, :, (g * R + r) * hd:(g * R + r + 1) * hd] * scale).astype(BF16)
                              for r in range(R)], axis=0)
        st = _nt(kg, qs) + neg
        m = jnp.max(st, axis=0, keepdims=True)
        if has_sink:
            sk = jnp.concatenate([jnp.broadcast_to(sink_ref[0:1, g * R + r:g * R + r + 1], (1, tq))
                                  for r in range(R)], axis=1)
            m = jnp.maximum(m, sk)
            e = jnp.exp(st - m)
            den = jnp.sum(e, axis=0, keepdims=True) + jnp.exp(sk - m)
        else:
            m = jnp.where(m == NEG_INF, 0.0, m)
            e = jnp.exp(st - m)
            den = jnp.maximum(jnp.sum(e, axis=0, keepdims=True), 1e-30)
        acc = _tn(vg, e.astype(BF16)) / den
        for r in range(R):
            h = g * R + r
            o = jnp.concatenate([acc[:, r * tq:(r + 1) * tq], pad], axis=0).T[:, :hd]
            if gate_ref is not None:
                o = o * _sigmoid(gate_ref[0, :, 3 * h + gate_col:3 * h + gate_col + 1])
            o_ref[0, :, h * hd:(h + 1) * hd] = o


def _band_attn(q3, k3, v3, *, H, G, window, sinks=None, gate3=None, gate_col=None):
    b, t, _ = q3.shape
    tq = min(QBLK, t)
    assert t % tq == 0
    span = min(window + tq, t)
    in_specs = [pl.BlockSpec((1, tq, H * HEAD_DIM), lambda i, j: (i, j, 0)),
                pl.BlockSpec((1, t, G * HEAD_DIM), lambda i, j: (i, 0, 0)),
                pl.BlockSpec((1, t, G * HEAD_DIM), lambda i, j: (i, 0, 0))]
    args = [q3, k3, v3]
    if sinks is not None:
        in_specs.append(pl.BlockSpec((1, LANES), lambda i, j: (0, 0)))
        args.append(jnp.pad(sinks.astype(F32), (0, LANES - H)).reshape(1, LANES))
    if gate3 is not None:
        in_specs.append(pl.BlockSpec((1, tq, LANES), lambda i, j: (i, j, 0)))
        args.append(gate3)
    else:
        gate_col = None
    return pl.pallas_call(
        functools.partial(_band_kernel, H=H, G=G, window=window, tq=tq, span=span, T=t,
                          has_sink=sinks is not None, gate_col=gate_col),
        grid=(b, t // tq), in_specs=in_specs,
        out_specs=pl.BlockSpec((1, tq, H * HEAD_DIM), lambda i, j: (i, j, 0)),
        out_shape=jax.ShapeDtypeStruct((b, t, H * HEAD_DIM), F32),
        compiler_params=_cparams("parallel", "parallel"), name="band_attn")(*args)


def _swin_kernel(*refs, H, G, W, has_sink, gate_col):
    q_ref, ck_ref, cv_ref, nk_ref, nv_ref = refs[:5]
    i = 5
    sink_ref = gate_ref = None
    if has_sink:
        sink_ref = refs[i]
        i += 1
    if gate_col is not None:
        gate_ref = refs[i]
        i += 1
    ok_ref, ov_ref, o_ref = refs[i:i + 3]
    row = lax.broadcasted_iota(jnp.int32, (W, G * HEAD_DIM), 0)
    kn = jnp.where(row == W - 1, nk_ref[0], pltpu.roll(ck_ref[0], W - 1, 0))
    vn = jnp.where(row == W - 1, nv_ref[0], pltpu.roll(cv_ref[0], W - 1, 0))
    ok_ref[0] = kn
    ov_ref[0] = vn
    R = H // G
    hd = HEAD_DIM
    scale = hd ** -0.5
    for g in range(G):
        kg = kn[:, g * hd:(g + 1) * hd].astype(BF16)
        vg = vn[:, g * hd:(g + 1) * hd].astype(BF16)
        qg = (q_ref[0, g * R:(g + 1) * R, :] * scale).astype(BF16)
        s = _nt(qg, kg)
        m = jnp.max(s, axis=-1, keepdims=True)
        if has_sink:
            sk = sink_ref[g * R:(g + 1) * R, :]
            m = jnp.maximum(m, sk)
            e = jnp.exp(s - m)
            den = jnp.sum(e, axis=-1, keepdims=True) + jnp.exp(sk - m)
        else:
            e = jnp.exp(s - m)
            den = jnp.maximum(jnp.sum(e, axis=-1, keepdims=True), 1e-30)
        o = _dot((e / den).astype(BF16), vg)
        if gate_ref is not None:
            o = o * _sigmoid(gate_ref[0, g * R:(g + 1) * R, gate_col:gate_col + 1])
        o_ref[0, g * R:(g + 1) * R, :] = o


def _step_window_attn(q, cache_k, cache_v, new_k, new_v, *, H, G, sinks=None, gate=None, gate_col=None):
    b, W, gd = cache_k.shape
    one = lambda i: (i, 0, 0)
    in_specs = [pl.BlockSpec((1, H, HEAD_DIM), one), pl.BlockSpec((1, W, gd), one),
                pl.BlockSpec((1, W, gd), one), pl.BlockSpec((1, 1, gd), one), pl.BlockSpec((1, 1, gd), one)]
    args = [q, cache_k, cache_v, new_k.reshape(b, 1, gd), new_v.reshape(b, 1, gd)]
    if sinks is not None:
        in_specs.append(pl.BlockSpec((H, 1), lambda i: (0, 0)))
        args.append(sinks.astype(F32).reshape(H, 1))
    if gate is not None:
        in_specs.append(pl.BlockSpec((1, H, 3), one))
        args.append(gate)
    else:
        gate_col = None
    return pl.pallas_call(
        functools.partial(_swin_kernel, H=H, G=G, W=W, has_sink=sinks is not None, gate_col=gate_col),
        grid=(b,), in_specs=in_specs,
        out_specs=[pl.BlockSpec((1, W, gd), one), pl.BlockSpec((1, W, gd), one),
                   pl.BlockSpec((1, H, HEAD_DIM), one)],
        out_shape=[jax.ShapeDtypeStruct((b, W, gd), F32), jax.ShapeDtypeStruct((b, W, gd), F32),
                   jax.ShapeDtypeStruct((b, H, HEAD_DIM), F32)],
        compiler_params=_cparams("parallel"), name="step_window_attn")(*args)


def _cmp_core(load_rows, pe_ref, wp_ref, w2_ref, gain_ref, out_ref, *, r0, rd):
    lane = lax.broadcasted_iota(jnp.int32, (1, LANES), 1)
    out = jnp.zeros((r0, NSA_KV_HEADS * HEAD_DIM), F32)
    for gp in range(2):
        acc = jnp.zeros((rd + SUBLANES, 2 * LANES), F32)
        for ip in range(CMP_STRIDE // 2):
            ab = jnp.concatenate([load_rows(2 * ip, gp), load_rows(2 * ip + 1, gp)], axis=1)
            lhs = jnp.concatenate([ab, pe_ref[ip]], axis=0).astype(BF16)
            acc = acc + _dot(lhs, wp_ref[ip])
        for gl in range(2):
            piece = acc[:, gl * LANES:(gl + 1) * LANES]
            bias = jnp.where(lane < HEAD_DIM, piece[rd:rd + 1], piece[rd + 1:rd + 2])
            data = piece[:rd] + bias
            pre = data + pltpu.roll(pltpu.roll(data, rd - 1, 0), HEAD_DIM, 1)
            hcat = _silu(pre)[:r0]
            out = out + _dot(hcat.astype(BF16), w2_ref[2 * gp + gl])
    if gain_ref is not None:
        n = NSA_KV_HEADS * HEAD_DIM
        r = lax.broadcasted_iota(jnp.int32, (n, n), 0) // HEAD_DIM
        c = lax.broadcasted_iota(jnp.int32, (n, n), 1) // HEAD_DIM
        bd = jnp.where(r == c, 1.0, 0.0).astype(BF16)
        x2 = out * out
        hi = x2.astype(BF16)
        lo = (x2 - hi.astype(F32)).astype(BF16)
        ss = _dot(hi, bd) + _dot(lo, bd)
        out = out * lax.rsqrt(ss * (1.0 / HEAD_DIM) + EPS) * gain_ref[...]
    out_ref[0] = out


def _cmp_seq_kernel(*refs, r0, has_norm):
    x_refs, (pe_ref, wp_ref, w2_ref) = refs[:2], refs[2:5]
    gain_ref = refs[5] if has_norm else None
    out_ref = refs[-1]

    def load_rows(i, gp):
        return x_refs[gp][0, pl.ds(i, r0, stride=CMP_STRIDE), :]

    _cmp_core(load_rows, pe_ref, wp_ref, w2_ref, gain_ref, out_ref, r0=r0, rd=r0)


def _cmp_paged_kernel(*refs, r0, n_pages, page, has_norm):
    pages = refs[1:1 + n_pages + 1]
    i = n_pages + 2
    pe_ref, wp_ref, w2_ref = refs[i:i + 3]
    gain_ref = refs[i + 3] if has_norm else None
    out_ref = refs[-5]
    bufs = (refs[-4:-2], refs[-2:])
    g = pl.program_id(0)
    rd = r0 + SUBLANES

    @pl.when(g == 0)
    def _():
        for ref in bufs[1]:
            ref[...] = jnp.zeros(ref.shape, F32)

    def step(fill, drain):
        for j, p in enumerate(pages):
            for gp in range(2):
                fill[gp][j * page:(j + 1) * page, :] = p[0, gp * LANES:(gp + 1) * LANES, :].T

        def load_rows(i, gp):
            return drain[gp][pl.ds(i, rd, stride=CMP_STRIDE), :]

        _cmp_core(load_rows, pe_ref, wp_ref, w2_ref, gain_ref, out_ref, r0=r0, rd=rd)

    @pl.when(g % 2 == 0)
    def _():
        step(bufs[0], bufs[1])

    @pl.when(g % 2 == 1)
    def _():
        step(bufs[1], bufs[0])


def _cmp_weights(pe, w1, w2):
    hd, G = HEAD_DIM, NSA_KV_HEADS
    npair = CMP_STRIDE // 2
    w1r = w1.reshape(2, npair, 2, hd, hd)
    wp = jnp.einsum('ab,hpidn->piadbhn', jnp.eye(2, dtype=F32), w1r)
    wp = wp.reshape(npair, 4 * hd, 4 * hd).astype(BF16)
    w2g = jnp.zeros((G, LANES, G * hd), F32)
    for g in range(G):
        w2g = w2g.at[g, :hd, g * hd:(g + 1) * hd].set(w2)
    per = jnp.broadcast_to(pe.reshape(2, npair, 2, 1, hd), (2, npair, 2, 2, hd))
    per = per.transpose(1, 0, 2, 3, 4).reshape(npair, 2, 4 * hd)
    pe_rows = jnp.pad(per.astype(F32), ((0, 0), (0, SUBLANES - 2), (0, 0)))
    return pe_rows, wp, w2g.astype(BF16)


def _cmp_specs(has_norm):
    gd = NSA_KV_HEADS * HEAD_DIM
    specs = [pl.BlockSpec((CMP_STRIDE // 2, SUBLANES, 4 * HEAD_DIM), lambda *a: (0, 0, 0)),
             pl.BlockSpec((CMP_STRIDE // 2, 4 * HEAD_DIM, 4 * HEAD_DIM), lambda *a: (0, 0, 0)),
             pl.BlockSpec((NSA_KV_HEADS, LANES, gd), lambda *a: (0, 0, 0))]
    if has_norm:
        specs.append(pl.BlockSpec((1, gd), lambda *a: (0, 0)))
    return specs


def _compress_seq(x3, pe, w1, w2, gain=None):
    b, t, gd = x3.shape
    r0 = t // CMP_STRIDE
    pe_rows, wp, w2g = _cmp_weights(pe, w1, w2)
    assert gd == 2 * LANES
    args = [x3, x3, pe_rows, wp, w2g]
    if gain is not None:
        args.append(jnp.tile(gain.reshape(1, HEAD_DIM), (1, NSA_KV_HEADS)))
    return pl.pallas_call(
        functools.partial(_cmp_seq_kernel, r0=r0, has_norm=gain is not None),
        grid=(b,),
        in_specs=[pl.BlockSpec((1, t, LANES), lambda i: (i, 0, 0)), pl.BlockSpec((1, t, LANES), lambda i: (i, 0, 1))]
        + _cmp_specs(gain is not None),
        out_specs=pl.BlockSpec((1, r0, gd), lambda i: (i, 0, 0)),
        out_shape=jax.ShapeDtypeStruct((b, r0, gd), F32),
        compiler_params=_cparams("parallel"), name="compress_seq")(*args)


def _pages_gd_tok(cache):
    n_pool, page, G, hd = cache.shape
    return jnp.transpose(cache, (0, 2, 3, 1)).reshape(n_pool, G * hd, page)


def _compress_paged(cache, page_table, pe, w1, w2, gain=None):
    n_pool, page, G, hd = cache.shape
    gd = G * hd
    b, n_pages = page_table.shape
    assert page == LANES and gd == 2 * LANES and n_pages % PAGES_PER_STEP == 0
    r0 = PAGES_PER_STEP * page // CMP_STRIDE
    nq = n_pages // PAGES_PER_STEP
    cv = _pages_gd_tok(cache)
    pe_rows, wp, w2g = _cmp_weights(pe, w1, w2)

    n_steps = b * nq
    n_in = PAGES_PER_STEP + 1
    step_page = jnp.minimum(jnp.arange(nq)[:, None] * PAGES_PER_STEP + jnp.arange(n_in)[None, :], n_pages - 1)
    flat = page_table[:, step_page].reshape(n_steps, n_in)
    flat = jnp.concatenate([flat, flat[-1:]], axis=0).reshape(-1).astype(jnp.int32)

    def page_spec(j):
        return pl.BlockSpec((1, gd, page), lambda g, pt: (pt[g * n_in + j], 0, 0))

    def out_map(g, pt):
        gm = jnp.maximum(g - 1, 0)
        return (gm // nq, gm % nq, 0)

    in_specs = [page_spec(j) for j in range(PAGES_PER_STEP + 1)] + _cmp_specs(gain is not None)
    args = [cv] * (PAGES_PER_STEP + 1) + [pe_rows, wp, w2g]
    if gain is not None:
        args.append(jnp.tile(gain.reshape(1, HEAD_DIM), (1, NSA_KV_HEADS)))
    gs = pltpu.PrefetchScalarGridSpec(
        num_scalar_prefetch=1, grid=(n_steps + 1,), in_specs=in_specs,
        out_specs=pl.BlockSpec((1, r0, gd), out_map),
        scratch_shapes=[pltpu.VMEM(((PAGES_PER_STEP + 1) * page, LANES), F32)] * 4)
    return pl.pallas_call(
        functools.partial(_cmp_paged_kernel, r0=r0, n_pages=PAGES_PER_STEP, page=page, has_norm=gain is not None),
        grid_spec=gs, out_shape=jax.ShapeDtypeStruct((b, nq * r0, gd), F32),
        compiler_params=_cparams("arbitrary"), name="compress_paged")(flat, *args)


def _rank_rows(imp_ref, n_cand):
    shape = imp_ref.shape
    x = imp_ref[...]
    ridx = lax.broadcasted_iota(jnp.int32, shape, 0)

    def body(sp, cnt):
        row = imp_ref[pl.ds(sp, 1), :]
        ge = jnp.where(row >= x, 1.0, 0.0)
        gt = jnp.where(row > x, 1.0, 0.0)
        return cnt + jnp.where(sp < ridx, ge, gt)

    return lax.fori_loop(0, n_cand, body, jnp.zeros(shape, F32))


def _rank_kernel(imp_ref, rank_ref, *, n_cand):
    rank_ref[...] = _rank_rows(imp_ref, n_cand)


def _rank_call(imp_t, n_cand):
    return pl.pallas_call(
        functools.partial(_rank_kernel, n_cand=n_cand),
        out_shape=jax.ShapeDtypeStruct(imp_t.shape, F32),
        compiler_params=pltpu.CompilerParams(vmem_limit_bytes=VMEM_LIMIT), name="rank")(imp_t)


def _cmpattn_kernel(q_ref, kc_ref, vc_ref, gate_ref, cov_ref, o_ref, sel_ref, imp_ref,
                    *, tq, n_cmp, n_sel, k_top):
    qi = pl.program_id(1)
    nc = kc_ref.shape[1]
    ns = cov_ref.shape[0]
    hd = HEAD_DIM
    G, R = NSA_KV_HEADS, NSA_HEADS // NSA_KV_HEADS
    scale = hd ** -0.5
    t_q = qi * tq + lax.broadcasted_iota(jnp.int32, (nc, tq), 1)
    cidx = lax.broadcasted_iota(jnp.int32, (nc, tq), 0)
    neg = jnp.where((cidx * CMP_STRIDE + CMP_BLK - 1 <= t_q) & (cidx < n_cmp), 0.0, NEG_INF)
    neg = jnp.concatenate([neg] * R, axis=1)
    sid = lax.broadcasted_iota(jnp.int32, (ns, tq), 0)
    tt = qi * tq + lax.broadcasted_iota(jnp.int32, (ns, tq), 1)
    bt = tt // SEL_BLK
    forced = (sid == 0) | (sid == bt) | (sid == bt - 1)
    valid = sid * SEL_BLK <= tt
    pad = jnp.zeros((LANES - hd, tq), F32)
    for g in range(G):
        kg = kc_ref[0, :, g * hd:(g + 1) * hd].astype(BF16)
        vg = vc_ref[0, :, g * hd:(g + 1) * hd].astype(BF16)
        qs = jnp.concatenate([(q_ref[0, :, (g * R + r) * hd:(g * R + r + 1) * hd] * scale).astype(BF16)
                              for r in range(R)], axis=0)
        st = _nt(kg, qs) + neg
        m = jnp.max(st, axis=0, keepdims=True)
        m = jnp.where(m == NEG_INF, 0.0, m)
        e = jnp.exp(st - m)
        p = e / jnp.maximum(jnp.sum(e, axis=0, keepdims=True), 1e-30)
        acc = _tn(vg, p.astype(BF16))
        psum = p[:, 0:tq]
        for r in range(1, R):
            psum = psum + p[:, r * tq:(r + 1) * tq]
        for r in range(R):
            h = g * R + r
            o = jnp.concatenate([acc[:, r * tq:(r + 1) * tq], pad], axis=0).T[:, :hd]
            o_ref[0, :, h * hd:(h + 1) * hd] = o * _sigmoid(gate_ref[0, :, 3 * h:3 * h + 1])
        p1, p2, p3 = _split3(psum)
        imp = _dot(cov_ref[...], p1) + _dot(cov_ref[...], p2) + _dot(cov_ref[...], p3)
        imp_ref[:, g * tq:(g + 1) * tq] = jnp.where(forced, SEL_BONUS, jnp.where(valid, imp, -SEL_BONUS))
    rank = _rank_rows(imp_ref, n_sel)
    for g in range(G):
        sel_ref[0, 0, g] = jnp.where(rank[:, g * tq:(g + 1) * tq] < k_top, 1.0, 0.0)


def _cover(n_cmp_rows, n_sel_cols):
    cstart = np.arange(n_cmp_rows)[:, None] * CMP_STRIDE
    sstart = np.arange(n_sel_cols)[None, :] * SEL_BLK
    return ((cstart < sstart + SEL_BLK) & (cstart + CMP_BLK > sstart)).astype(np.float32)


def _cmp_attn_seq(q3, kcmp, vcmp, gate3):
    b, t, _ = q3.shape
    tq = min(QBLK, t)
    nc = kcmp.shape[1]
    n_cmp = (t - CMP_BLK) // CMP_STRIDE + 1
    n_sel = -(-t // SEL_BLK)
    k_top = min(N_SEL, n_sel)
    cov_t = jnp.asarray(_cover(nc, n_sel).T, BF16)
    gd = NSA_KV_HEADS * HEAD_DIM
    return pl.pallas_call(
        functools.partial(_cmpattn_kernel, tq=tq, n_cmp=n_cmp, n_sel=n_sel, k_top=k_top),
        grid=(b, t // tq),
        in_specs=[pl.BlockSpec((1, tq, NSA_HEADS * HEAD_DIM), lambda i, j: (i, j, 0)),
                  pl.BlockSpec((1, nc, gd), lambda i, j: (i, 0, 0)),
                  pl.BlockSpec((1, nc, gd), lambda i, j: (i, 0, 0)),
                  pl.BlockSpec((1, tq, LANES), lambda i, j: (i, j, 0)),
                  pl.BlockSpec((n_sel, nc), lambda i, j: (0, 0))],
        out_specs=[pl.BlockSpec((1, tq, NSA_HEADS * HEAD_DIM), lambda i, j: (i, j, 0)),
                   pl.BlockSpec((1, 1, NSA_KV_HEADS, n_sel, tq), lambda i, j: (i, j, 0, 0, 0))],
        out_shape=[jax.ShapeDtypeStruct((b, t, NSA_HEADS * HEAD_DIM), F32),
                   jax.ShapeDtypeStruct((b, t // tq, NSA_KV_HEADS, n_sel, tq), F32)],
        scratch_shapes=[pltpu.VMEM((n_sel, NSA_KV_HEADS * tq), F32)],
        compiler_params=_cparams("parallel", "parallel"), name="cmp_attn_seq")(q3, kcmp, vcmp, gate3, cov_t)


def _cmpstep_kernel(q_ref, kc_ref, vc_ref, gate_ref, cov_ref, o_ref, imp_ref, *, n_cmp, n_sel, pos):
    nc = kc_ref.shape[1]
    ns = cov_ref.shape[1]
    hd = HEAD_DIM
    G, R = NSA_KV_HEADS, NSA_HEADS // NSA_KV_HEADS
    scale = hd ** -0.5
    cidx = lax.broadcasted_iota(jnp.int32, (R, nc), 1)
    mask = (cidx * CMP_STRIDE + CMP_BLK - 1 <= pos) & (cidx < n_cmp)
    psums = []
    for g in range(G):
        kg = kc_ref[0, :, g * hd:(g + 1) * hd].astype(BF16)
        vg = vc_ref[0, :, g * hd:(g + 1) * hd].astype(BF16)
        qg = (q_ref[0, g * R:(g + 1) * R, :] * scale).astype(BF16)
        s = jnp.where(mask, _nt(qg, kg), NEG_INF)
        m = jnp.max(s, axis=-1, keepdims=True)
        m = jnp.where(m == NEG_INF, 0.0, m)
        e = jnp.exp(s - m)
        p = e / jnp.maximum(jnp.sum(e, axis=-1, keepdims=True), 1e-30)
        o = _dot(p.astype(BF16), vg)
        o_ref[0, g * R:(g + 1) * R, :] = o * _sigmoid(gate_ref[0, g * R:(g + 1) * R, 0:1])
        psums.append(jnp.sum(p, axis=0, keepdims=True))
    p1, p2, p3 = _split3(jnp.concatenate(psums, axis=0))
    imp = _dot(p1, cov_ref[...]) + _dot(p2, cov_ref[...]) + _dot(p3, cov_ref[...])
    sid = lax.broadcasted_iota(jnp.int32, (G, ns), 1)
    bt = pos // SEL_BLK
    forced = (sid == 0) | (sid == bt) | (sid == bt - 1)
    valid = sid * SEL_BLK <= pos
    imp_ref[0] = jnp.where(forced, SEL_BONUS, jnp.where(valid, imp, -SEL_BONUS))


def _cmp_attn_step(q3, kcmp, vcmp, gate3, *, pos, n_cmp, n_sel, ns_pad):
    b = q3.shape[0]
    nc = kcmp.shape[1]
    cov = jnp.asarray(_cover(nc, ns_pad), BF16)
    gd = NSA_KV_HEADS * HEAD_DIM
    one = lambda i: (i, 0, 0)
    return pl.pallas_call(
        functools.partial(_cmpstep_kernel, n_cmp=n_cmp, n_sel=n_sel, pos=pos),
        grid=(b,),
        in_specs=[pl.BlockSpec((1, NSA_HEADS, HEAD_DIM), one), pl.BlockSpec((1, nc, gd), one),
                  pl.BlockSpec((1, nc, gd), one), pl.BlockSpec((1, NSA_HEADS, 3), one),
                  pl.BlockSpec((nc, ns_pad), lambda i: (0, 0))],
        out_specs=[pl.BlockSpec((1, NSA_HEADS, HEAD_DIM), one), pl.BlockSpec((1, NSA_KV_HEADS, ns_pad), one)],
        out_shape=[jax.ShapeDtypeStruct((b, NSA_HEADS, HEAD_DIM), F32),
                   jax.ShapeDtypeStruct((b, NSA_KV_HEADS, ns_pad), F32)],
        compiler_params=_cparams("parallel"), name="cmp_attn_step")(q3, kcmp, vcmp, gate3, cov)


def _selattn_kernel(q_ref, k_ref, v_ref, sel_ref, gate_ref, o_ref, qs_ref, m_ref, l_ref, acc_ref,
                    *, tq, tk, n_sel):
    qi = pl.program_id(1)
    hd = HEAD_DIM
    G, R = NSA_KV_HEADS, NSA_HEADS // NSA_KV_HEADS
    scale = hd ** -0.5
    n_tiles = ((qi + 1) * tq + tk - 1) // tk
    tt = qi * tq + lax.broadcasted_iota(jnp.int32, (tk, tq), 1)
    krow = lax.broadcasted_iota(jnp.int32, (tk, tq), 0)
    pad = jnp.zeros((LANES - hd, tq), F32)
    for g in range(G):
        for r in range(R):
            h = g * R + r
            qs_ref[r * tq:(r + 1) * tq, :] = (q_ref[0, :, h * hd:(h + 1) * hd] * scale).astype(BF16)
        m_ref[...] = jnp.full(m_ref.shape, NEG_INF, F32)
        l_ref[...] = jnp.zeros(l_ref.shape, F32)
        acc_ref[...] = jnp.zeros(acc_ref.shape, F32)

        def body(j, carry):
            k0 = pl.multiple_of(j * tk, tk)
            kt = k_ref[0, pl.ds(k0, tk), g * hd:(g + 1) * hd].astype(BF16)
            vt = v_ref[0, pl.ds(k0, tk), g * hd:(g + 1) * hd].astype(BF16)
            st = _nt(kt, qs_ref[...])
            b0 = j * (tk // SEL_BLK)
            chosen = jnp.concatenate(
                [jnp.broadcast_to(sel_ref[0, 0, g, pl.ds(b0 + i, 1), :], (SEL_BLK, tq))
                 for i in range(tk // SEL_BLK)], axis=0)
            neg = jnp.where((chosen > 0.5) & (k0 + krow <= tt), 0.0, NEG_INF)
            st = st + jnp.concatenate([neg] * R, axis=1)
            m_old = m_ref[...]
            m_new = jnp.maximum(m_old, jnp.max(st, axis=0, keepdims=True))
            m_safe = jnp.where(m_new == NEG_INF, 0.0, m_new)
            alpha = jnp.exp(m_old - m_safe)
            p = jnp.exp(st - m_safe)
            l_ref[...] = alpha * l_ref[...] + jnp.sum(p, axis=0, keepdims=True)
            acc_ref[...] = alpha * acc_ref[...] + _tn(vt, p.astype(BF16))
            m_ref[...] = m_new
            return carry

        lax.fori_loop(0, n_tiles, body, 0)
        accn = acc_ref[...] / jnp.maximum(l_ref[...], 1e-30)
        for r in range(R):
            h = g * R + r
            o = jnp.concatenate([accn[:, r * tq:(r + 1) * tq], pad], axis=0).T[:, :hd]
            o_ref[0, :, h * hd:(h + 1) * hd] = o * _sigmoid(gate_ref[0, :, 3 * h + 1:3 * h + 2])


def _sel_attn_seq(q3, k3, v3, sel, gate3):
    b, t, _ = q3.shape
    tq = min(QBLK, t)
    tk = min(8 * QBLK, t)
    n_sel = sel.shape[-2]
    R = NSA_HEADS // NSA_KV_HEADS
    gd = NSA_KV_HEADS * HEAD_DIM
    return pl.pallas_call(
        functools.partial(_selattn_kernel, tq=tq, tk=tk, n_sel=n_sel),
        grid=(b, t // tq),
        in_specs=[pl.BlockSpec((1, tq, NSA_HEADS * HEAD_DIM), lambda i, j: (i, j, 0)),
                  pl.BlockSpec((1, t, gd), lambda i, j: (i, 0, 0)),
                  pl.BlockSpec((1, t, gd), lambda i, j: (i, 0, 0)),
                  pl.BlockSpec((1, 1, NSA_KV_HEADS, n_sel, tq), lambda i, j: (i, j, 0, 0, 0)),
                  pl.BlockSpec((1, tq, LANES), lambda i, j: (i, j, 0))],
        out_specs=pl.BlockSpec((1, tq, NSA_HEADS * HEAD_DIM), lambda i, j: (i, j, 0)),
        out_shape=jax.ShapeDtypeStruct((b, t, NSA_HEADS * HEAD_DIM), F32),
        scratch_shapes=[pltpu.VMEM((R * tq, HEAD_DIM), BF16), pltpu.VMEM((1, R * tq), F32),
                        pltpu.VMEM((1, R * tq), F32), pltpu.VMEM((HEAD_DIM, R * tq), F32)],
        compiler_params=_cparams("parallel", "parallel"), name="sel_attn_seq")(q3, k3, v3, sel, gate3)


def _selstep_kernel(*refs, n_slots, page):
    idx_ref, half_ref, skip_ref, hasnew_ref = refs[:4]
    q_ref = refs[4]
    k_refs = refs[5:5 + n_slots]
    v_refs = refs[5 + n_slots:5 + 2 * n_slots]
    nk_ref, nv_ref, gate_ref, o_ref = refs[5 + 2 * n_slots:]
    b = pl.program_id(0)
    g = pl.program_id(1)
    base = (b * NSA_KV_HEADS + g) * n_slots
    hd = HEAD_DIM
    R = NSA_HEADS // NSA_KV_HEADS
    scale = hd ** -0.5
    qg = (q_ref[0, 0] * scale).astype(BF16)
    lane_half = lax.broadcasted_iota(jnp.int32, (R, page), 1) // SEL_BLK
    nk = nk_ref[0, 0]
    nv = nv_ref[0, 0]
    s_new = _nt(qg, jnp.broadcast_to(nk, (SUBLANES, hd)).astype(BF16))[:, 0:1]
    s_new = jnp.where(hasnew_ref[b * NSA_KV_HEADS + g] == 1, s_new, NEG_INF)
    m = s_new
    scores = []
    for j in range(n_slots):
        s = _dot(qg, k_refs[j][0, 0].astype(BF16))
        s = jnp.where(lane_half == half_ref[base + j], s, NEG_INF)
        s = jnp.where(skip_ref[base + j] == 0, s, NEG_INF)
        m = jnp.maximum(m, jnp.max(s, axis=-1, keepdims=True))
        scores.append(s)
    m = jnp.where(m == NEG_INF, 0.0, m)
    e_new = jnp.exp(s_new - m)
    den = e_new
    es = []
    for s in scores:
        e = jnp.exp(s - m)
        den = den + jnp.sum(e, axis=-1, keepdims=True)
        es.append(e)
    den = jnp.maximum(den, 1e-30)
    o = (e_new / den) * nv
    for e, v_ref in zip(es, v_refs):
        o = o + _nt((e / den).astype(BF16), v_ref[0, 0].astype(BF16))
    o_ref[0, 0] = o * _sigmoid(gate_ref[0, 0, :, 1:2])


def _sel_attn_step(q4, cache_k, cache_v, page_idx, half, skip, has_new, new_k, new_v, gate4):
    b = q4.shape[0]
    n_pool, page, G, hd = cache_k.shape
    R = NSA_HEADS // NSA_KV_HEADS
    ck = _pages_gd_tok(cache_k).reshape(n_pool, G, hd, page)
    cv = _pages_gd_tok(cache_v).reshape(n_pool, G, hd, page)
    n_slots = N_SEL

    def slot_spec(j):
        return pl.BlockSpec((1, 1, hd, page),
                            lambda i, g, idx, hf, sk, hn: (idx[(i * G + g) * n_slots + j], g, 0, 0))

    grp = lambda i, g, idx, hf, sk, hn: (i, g, 0, 0)
    in_specs = ([pl.BlockSpec((1, 1, R, hd), grp)] + [slot_spec(j) for j in range(n_slots)] * 2
                + [pl.BlockSpec((1, 1, 1, hd), grp)] * 2 + [pl.BlockSpec((1, 1, R, 3), grp)])
    gs = pltpu.PrefetchScalarGridSpec(
        num_scalar_prefetch=4, grid=(b, G), in_specs=in_specs,
        out_specs=pl.BlockSpec((1, 1, R, hd), grp))
    return pl.pallas_call(
        functools.partial(_selstep_kernel, n_slots=n_slots, page=page), grid_spec=gs,
        out_shape=jax.ShapeDtypeStruct((b, G, R, hd), F32),
        compiler_params=_cparams("arbitrary", "arbitrary"), name="sel_attn_step")(
            page_idx, half, skip, has_new, q4, *([ck] * n_slots), *([cv] * n_slots), new_k, new_v, gate4)


def _mem_kernel(q_ref, k_ref, v_ref, g_ref, o_ref, *, split_heads):
    hd = MEM_HEAD_DIM
    for h in range(MEM_HEADS):
        sl = slice(h * hd, (h + 1) * hd)
        kh = k_ref[0, 0, :, h, :] if split_heads else k_ref[0, :, sl]
        vh = v_ref[0, 0, :, h, :] if split_heads else v_ref[0, :, sl]
        qh = q_ref[0, :, sl]
        qn = qh * lax.rsqrt(jnp.mean(qh * qh, axis=-1, keepdims=True) + EPS) * g_ref[...]
        s = _nt(qn.astype(BF16), kh.astype(BF16)) * (hd ** -0.5)
        m = jnp.max(s, axis=-1, keepdims=True)
        e = jnp.exp(s - m)
        p = e / jnp.sum(e, axis=-1, keepdims=True)
        o_ref[0, :, sl] = _dot(p.astype(BF16), vh.astype(BF16))


def _mem_attn(q3, k, v, gq, layer=None):
    b, t, wd = q3.shape
    tm = _row_tile(t, 512)
    split = layer is not None
    if split:
        mt = k.shape[2]
        kv_spec = pl.BlockSpec((1, 1, mt, MEM_HEADS, MEM_HEAD_DIM), lambda i, j: (layer, i, 0, 0, 0))
    else:
        mt = k.shape[1]
        kv_spec = pl.BlockSpec((1, mt, wd), lambda i, j: (i, 0, 0))
    return pl.pallas_call(
        functools.partial(_mem_kernel, split_heads=split), grid=(b, t // tm),
        in_specs=[pl.BlockSpec((1, tm, wd), lambda i, j: (i, j, 0)), kv_spec, kv_spec,
                  pl.BlockSpec((1, MEM_HEAD_DIM), lambda i, j: (0, 0))],
        out_specs=pl.BlockSpec((1, tm, wd), lambda i, j: (i, j, 0)),
        out_shape=jax.ShapeDtypeStruct((b, t, wd), F32),
        compiler_params=_cparams("parallel", "parallel"), name="mem_attn")(
            q3, k, v, gq.reshape(1, MEM_HEAD_DIM))


FFN_COLS = 256


def _ffn_kernel(*refs, ff, state_mode):
    if state_mode:
        u_ref, p2_ref, p1_ref, cw_ref, cb_ref, wo_ref, x_ref, o_ref, acc_ref = refs
    else:
        u_ref, halo_ref, cw_ref, cb_ref, wo_ref, x_ref, o_ref, acc_ref = refs
        first = pl.program_id(1) == 0

    def conv(off):
        sl = slice(off, off + FFN_COLS)
        if state_mode:
            cur, s1, s2 = u_ref[:, sl], p1_ref[:, sl], p2_ref[:, sl]
        else:
            cur = u_ref[0, :, sl]
            halo = jnp.where(first, 0.0, halo_ref[0, :, sl])
            s1 = _shift_rows(cur, halo, 1)
            s2 = _shift_rows(cur, halo, 2)
        return s2 * cw_ref[0:1, sl] + s1 * cw_ref[1:2, sl] + cur * cw_ref[2:3, sl] + cb_ref[:, sl]

    acc_ref[...] = x_ref[...] if state_mode else x_ref[0]
    for c in range(ff // FFN_COLS):
        a = conv(c * FFN_COLS)
        b = conv(ff + c * FFN_COLS)
        act = (_silu(a) * b).astype(BF16)
        acc_ref[...] += _dot(act, wo_ref[c * FFN_COLS:(c + 1) * FFN_COLS, :])
    if state_mode:
        o_ref[...] = acc_ref[...]
    else:
        o_ref[0] = acc_ref[...]


def _ffn_tail_seq(u3, cw8, cb, wo, x3):
    b, t, f2 = u3.shape
    ff = f2 // 2
    d = x3.shape[-1]
    assert ff % FFN_COLS == 0
    tm = _row_tile(t, 256)
    hb = tm // SUBLANES
    return pl.pallas_call(
        functools.partial(_ffn_kernel, ff=ff, state_mode=False),
        grid=(b, t // tm),
        in_specs=[pl.BlockSpec((1, tm, f2), lambda i, j: (i, j, 0)),
                  pl.BlockSpec((1, SUBLANES, f2), lambda i, j: (i, jnp.maximum(j * hb - 1, 0), 0)),
                  pl.BlockSpec((SUBLANES, f2), lambda i, j: (0, 0)),
                  pl.BlockSpec((1, f2), lambda i, j: (0, 0)),
                  pl.BlockSpec((ff, d), lambda i, j: (0, 0)),
                  pl.BlockSpec((1, tm, d), lambda i, j: (i, j, 0))],
        out_specs=pl.BlockSpec((1, tm, d), lambda i, j: (i, j, 0)),
        out_shape=jax.ShapeDtypeStruct((b, t, d), F32),
        scratch_shapes=[pltpu.VMEM((tm, d), F32)],
        compiler_params=_cparams("parallel", "arbitrary"), name="ffn_tail_seq")(u3, u3, cw8, cb, wo, x3)


def _ffn_tail_step(u, prev2, prev1, cw8, cb, wo, x):
    m, f2 = u.shape
    ff = f2 // 2
    d = x.shape[-1]
    tm = _row_tile(m, 128)
    row = pl.BlockSpec((tm, f2), lambda i: (i, 0))
    return pl.pallas_call(
        functools.partial(_ffn_kernel, ff=ff, state_mode=True),
        grid=(m // tm,),
        in_specs=[row, row, row,
                  pl.BlockSpec((SUBLANES, f2), lambda i: (0, 0)), pl.BlockSpec((1, f2), lambda i: (0, 0)),
                  pl.BlockSpec((ff, d), lambda i: (0, 0)), pl.BlockSpec((tm, d), lambda i: (i, 0))],
        out_specs=pl.BlockSpec((tm, d), lambda i: (i, 0)),
        out_shape=jax.ShapeDtypeStruct((m, d), F32),
        scratch_shapes=[pltpu.VMEM((tm, d), F32)],
        compiler_params=_cparams("parallel"), name="ffn_tail_step")(u, prev2, prev1, cw8, cb, wo, x)


def _pad_rows(a, rows):
    return jnp.pad(a.astype(F32), ((0, rows - a.shape[0]), (0, 0)))


def _pad_cols(w, cols):
    return jnp.pad(w, ((0, 0), (0, cols - w.shape[1])))


def _mem_block(x, bsz, t, mem_k, mem_v, P, layer, cache_layer=None):
    qm = _proj([x], P['mem_w_q'][layer], gain=P['mem_norm_x'][layer])
    om = _mem_attn(qm.reshape(bsz, t, -1), mem_k, mem_v, P['mem_q_norm'][layer], cache_layer)
    return _proj([om.reshape(bsz * t, -1)], P['mem_w_o'][layer], res=x)


def _ffn_in(x, P, layer):
    f2 = P['ffn_w_in'][layer].shape[1]
    tn = f2 // 4 if (f2 // 4) % LANES == 0 else f2
    return _proj([x], P['ffn_w_in'][layer], gain=P['ffn_norm'][layer], tn=tn)


def _dn_params(P):
    pa = jnp.zeros((1, LANES), F32).at[0, DN_HEADS:2 * DN_HEADS].set(P['l0_dn_a_log'].astype(F32))
    pd = jnp.zeros((1, LANES), F32).at[0, DN_HEADS:2 * DN_HEADS].set(P['l0_dn_dt_bias'].astype(F32))
    return _pad_rows(P['l0_dn_conv_w'], SUBLANES), pa, pd, P['l0_dn_norm'].astype(F32).reshape(1, DN_DK)


L0_SEG = (1536, 512, LANES, 512, 128, 128)
L1_SEG = (1024, 256, 256, 256, 256, 256, 256, LANES)


def _prompt_group(x_prompt, mem_prompt, P):
    bsz, t, d = x_prompt.shape
    m = bsz * t
    x = x_prompt.reshape(m, d)
    tabs = _rope_tables(jnp.arange(t, dtype=jnp.int32))

    mt = mem_prompt.shape[1]
    mem_rows = mem_prompt.reshape(bsz * mt, d)
    mem_k, mem_v = [], []
    for layer in range(2):
        kraw, v = _proj([mem_rows], P['mem_w_kv'][layer], gain=P['mem_norm_kv'][layer],
                        seg=(MEM_HEADS * MEM_HEAD_DIM,) * 2)
        mem_k.append(_headnorm(kraw, P['mem_k_norm'][layer], MEM_HEAD_DIM).reshape(bsz, mt, -1))
        mem_v.append(v.reshape(bsz, mt, -1))

    qkv, z, ba, q_b, k_b, v_b = _proj([x], P['l0_w_in'], gain=P['mix_norm'][0], seg=L0_SEG)
    cw8, pa, pd, gn = _dn_params(P)
    qkv3 = qkv.reshape(bsz, t, -1)
    w, u, qg, kd, qk, gc = _dn_prepare(qkv3, ba.reshape(bsz, t, LANES), cw8, pa, pd)
    o_dn, s_fin = _dn_recur(w, u, qg, kd, qk, gc, z.reshape(bsz, t, -1), gn)
    qs = _headnorm(q_b, P['l0_swa_q_norm'], HEAD_DIM, tabs=tabs, seq_len=t, want_norm=False)
    ks = _headnorm(k_b, P['l0_swa_k_norm'], HEAD_DIM, tabs=tabs, seq_len=t, want_norm=False)
    ks3 = ks.reshape(bsz, t, -1)
    vs3 = v_b.reshape(bsz, t, -1)
    o_sw = _band_attn(qs.reshape(bsz, t, -1), ks3, vs3, H=SWA_HEADS, G=SWA_KV_HEADS, window=SWA_WINDOW,
                      sinks=P['l0_swa_sinks'])
    mix = jnp.concatenate([o_dn, o_sw], axis=-1).reshape(m, -1)
    x = _proj([mix], P['l0_w_out'], res=x)
    x = _mem_block(x, bsz, t, mem_k[0], mem_v[0], P, 0)
    u0 = _ffn_in(x, P, 0)
    u03 = u0.reshape(bsz, t, -1)
    x = _ffn_tail_seq(u03, P['ffn_cw8'][0], P['ffn_cb'][0], P['ffn_w_out'][0], x.reshape(bsz, t, d)).reshape(m, d)
    nb = min(SWA_WINDOW, t)
    ab_state = (s_fin, qkv3[:, t - 3:], ks3[:, t - nb:].reshape(bsz, nb, SWA_KV_HEADS, HEAD_DIM),
                vs3[:, t - nb:].reshape(bsz, nb, SWA_KV_HEADS, HEAD_DIM))

    q, kc, vc, ksr, vsel, kwr, vw, gr = _proj([x], P['l1_w_in'], gain=P['mix_norm'][1], seg=L1_SEG)
    qn, qr = _headnorm(q, P['l1_q_norm'], HEAD_DIM, tabs=tabs, seq_len=t)
    ksel = _headnorm(ksr, P['l1_k_norm_sel'], HEAD_DIM, tabs=tabs, seq_len=t, want_norm=False)
    kw = _headnorm(kwr, P['l1_k_norm_win'], HEAD_DIM, tabs=tabs, seq_len=t, want_norm=False)
    gd = NSA_KV_HEADS * HEAD_DIM
    r3 = lambda a: a.reshape(bsz, t, -1)
    kcmp = _compress_seq(r3(kc), P['l1_cmp_pe_k'], P['l1_cmp_w1_k'], P['l1_cmp_w2_k'], P['l1_k_norm_cmp'])
    vcmp = _compress_seq(r3(vc), P['l1_cmp_pe_v'], P['l1_cmp_w1_v'], P['l1_cmp_w2_v'])
    gr3 = r3(gr)
    o_c, sel = _cmp_attn_seq(r3(qn), kcmp, vcmp, gr3)
    o_s = _sel_attn_seq(r3(qr), r3(ksel), r3(vsel), sel, gr3)
    o_w = _band_attn(r3(qr), r3(kw), r3(vw), H=NSA_HEADS, G=NSA_KV_HEADS, window=NSA_WINDOW,
                     gate3=gr3, gate_col=2)
    x = _proj([o_c.reshape(m, -1), o_s.reshape(m, -1), o_w.reshape(m, -1)], P['l1_w_out'], res=x)
    x = _mem_block(x, bsz, t, mem_k[1], mem_v[1], P, 1)
    u1 = _ffn_in(x, P, 1)
    u13 = u1.reshape(bsz, t, -1)
    x = _ffn_tail_seq(u13, P['ffn_cw8'][1], P['ffn_cb'][1], P['ffn_w_out'][1], x.reshape(bsz, t, d)).reshape(m, d)
    nw = min(NSA_WINDOW, t)
    r4 = lambda a: a.reshape(bsz, t, NSA_KV_HEADS, HEAD_DIM)
    nsa_state = (r4(kc), r4(vc), r4(ksel), r4(vsel), r4(kw)[:, t - nw:], r4(vw)[:, t - nw:])
    mem_kr = jnp.stack([k.reshape(bsz, mt, MEM_HEADS, MEM_HEAD_DIM) for k in mem_k])
    mem_vr = jnp.stack([v.reshape(bsz, mt, MEM_HEADS, MEM_HEAD_DIM) for v in mem_v])
    ffn_state = jnp.stack([u03[:, t - 2:], u13[:, t - 2:]])
    return x.reshape(bsz, t, d), ab_state, nsa_state, mem_kr, mem_vr, ffn_state


def _sample_group(x_sample, state_dn, state_dn_conv, cache_swa_k, cache_swa_v, cache_cmp_k, cache_cmp_v,
                  cache_sel_k, cache_sel_v, cache_win_k, cache_win_v, cache_mem_k, cache_mem_v,
                  state_ffn_conv, page_table, P):
    bsz, t, d = x_sample.shape
    assert t == 1
    page = cache_cmp_k.shape[1]
    n_pages = page_table.shape[1]
    past = n_pages * page
    assert cache_swa_k.shape[1] == SWA_WINDOW and cache_win_k.shape[1] == NSA_WINDOW
    x = x_sample.reshape(bsz, d)
    tabs = _rope_tables(jnp.full((1,), past, jnp.int32))

    qkv, z, ba, q_b, k_b, v_b = _proj([x], P['l0_w_in'], gain=P['mix_norm'][0], seg=L0_SEG)
    cw8, pa, pd, gn = _dn_params(P)
    o_dn, s_new, conv_new = _dn_step(qkv, state_dn_conv, ba, z, state_dn, cw8, pa, pd, gn)
    qs = _headnorm(q_b, P['l0_swa_q_norm'], HEAD_DIM, tabs=tabs, want_norm=False)
    ks = _headnorm(k_b, P['l0_swa_k_norm'], HEAD_DIM, tabs=tabs, want_norm=False)
    swa_k, swa_v, o_sw = _step_window_attn(
        qs.reshape(bsz, SWA_HEADS, HEAD_DIM), cache_swa_k.reshape(bsz, SWA_WINDOW, -1),
        cache_swa_v.reshape(bsz, SWA_WINDOW, -1), ks, v_b, H=SWA_HEADS, G=SWA_KV_HEADS,
        sinks=P['l0_swa_sinks'])
    mix = jnp.concatenate([o_dn.reshape(bsz, -1), o_sw.reshape(bsz, -1)], axis=-1)
    x = _proj([mix], P['l0_w_out'], res=x)
    x = _mem_block(x, bsz, 1, cache_mem_k, cache_mem_v, P, 0, cache_layer=0)
    u0 = _ffn_in(x, P, 0)
    x = _ffn_tail_step(u0, state_ffn_conv[0, :, 0], state_ffn_conv[0, :, 1], P['ffn_cw8'][0], P['ffn_cb'][0],
                       P['ffn_w_out'][0], x)
    ab_state = (s_new, conv_new, swa_k.reshape(bsz, SWA_WINDOW, SWA_KV_HEADS, HEAD_DIM),
                swa_v.reshape(bsz, SWA_WINDOW, SWA_KV_HEADS, HEAD_DIM))

    q, kc, vc, ksr, vsel, kwr, vw, gr = _proj([x], P['l1_w_in'], gain=P['mix_norm'][1], seg=L1_SEG)
    qn, qr = _headnorm(q, P['l1_q_norm'], HEAD_DIM, tabs=tabs)
    ksel = _headnorm(ksr, P['l1_k_norm_sel'], HEAD_DIM, tabs=tabs, want_norm=False)
    kw = _headnorm(kwr, P['l1_k_norm_win'], HEAD_DIM, tabs=tabs, want_norm=False)
    G, R = NSA_KV_HEADS, NSA_HEADS // NSA_KV_HEADS
    total = past + 1
    n_cmp = (total - CMP_BLK) // CMP_STRIDE + 1
    n_sel = -(-total // SEL_BLK)
    ns_pad = -(-n_sel // LANES) * LANES
    kcmp = _compress_paged(cache_cmp_k, page_table, P['l1_cmp_pe_k'], P['l1_cmp_w1_k'], P['l1_cmp_w2_k'],
                           P['l1_k_norm_cmp'])
    vcmp = _compress_paged(cache_cmp_v, page_table, P['l1_cmp_pe_v'], P['l1_cmp_w1_v'], P['l1_cmp_w2_v'])
    gate_h = gr[:, :3 * NSA_HEADS].reshape(bsz, NSA_HEADS, 3)
    o_c, imp = _cmp_attn_step(qn.reshape(bsz, NSA_HEADS, HEAD_DIM), kcmp, vcmp, gate_h, pos=past, n_cmp=n_cmp,
                              n_sel=n_sel, ns_pad=ns_pad)
    imp_t = imp.reshape(bsz * G, ns_pad).T
    rank = _rank_call(imp_t, n_sel)
    rank = jnp.where(jnp.arange(ns_pad)[:, None] < n_sel, rank, 1e9)
    blocks = jnp.argsort(rank, axis=0)[:N_SEL].T.astype(jnp.int32)
    blocks = blocks.reshape(bsz, G, N_SEL)
    per_page = page // SEL_BLK
    is_new = blocks >= n_pages * per_page
    blk_c = jnp.minimum(blocks, n_pages * per_page - 1)
    page_idx = jnp.take_along_axis(page_table[:, None, :], blk_c // per_page, axis=2)
    o_s = _sel_attn_step(
        qr.reshape(bsz, G, R, HEAD_DIM), cache_sel_k, cache_sel_v, page_idx.reshape(-1).astype(jnp.int32),
        (blk_c % per_page).reshape(-1).astype(jnp.int32), is_new.reshape(-1).astype(jnp.int32),
        jnp.any(is_new, axis=-1).reshape(-1).astype(jnp.int32), ksel.reshape(bsz, G, 1, HEAD_DIM),
        vsel.reshape(bsz, G, 1, HEAD_DIM), gate_h.reshape(bsz, G, R, 3))
    win_k, win_v, o_w = _step_window_attn(
        qr.reshape(bsz, NSA_HEADS, HEAD_DIM), cache_win_k.reshape(bsz, NSA_WINDOW, -1),
        cache_win_v.reshape(bsz, NSA_WINDOW, -1), kw, vw, H=NSA_HEADS, G=NSA_KV_HEADS,
        gate=gate_h, gate_col=2)
    x = _proj([o_c.reshape(bsz, -1), o_s.reshape(bsz, -1), o_w.reshape(bsz, -1)], P['l1_w_out'], res=x)
    x = _mem_block(x, bsz, 1, cache_mem_k, cache_mem_v, P, 1, cache_layer=1)
    u1 = _ffn_in(x, P, 1)
    x = _ffn_tail_step(u1, state_ffn_conv[1, :, 0], state_ffn_conv[1, :, 1], P['ffn_cw8'][1], P['ffn_cb'][1],
                       P['ffn_w_out'][1], x)
    r4 = lambda a: a.reshape(bsz, 1, G, HEAD_DIM)
    nsa_state = (r4(kc), r4(vc), r4(ksel), r4(vsel), win_k.reshape(bsz, NSA_WINDOW, G, HEAD_DIM),
                 win_v.reshape(bsz, NSA_WINDOW, G, HEAD_DIM))
    ffn_state = jnp.stack([jnp.stack([state_ffn_conv[0, :, 1], u0], axis=1),
                           jnp.stack([state_ffn_conv[1, :, 1], u1], axis=1)])
    return x.reshape(bsz, 1, d), ab_state, nsa_state, ffn_state


def kernel(x_prompt, x_sample, state_dn, state_dn_conv, cache_swa_k, cache_swa_v, cache_cmp_k, cache_cmp_v, cache_sel_k, cache_sel_v, cache_win_k, cache_win_v, cache_mem_k, cache_mem_v, state_ffn_conv, page_table, mem_prompt, mix_norm, l0_w_in, l0_dn_conv_w, l0_dn_a_log, l0_dn_dt_bias, l0_dn_norm, l0_swa_q_norm, l0_swa_k_norm, l0_swa_sinks, l0_w_out, l1_w_in, l1_q_norm, l1_k_norm_cmp, l1_k_norm_sel, l1_k_norm_win, l1_cmp_pe_k, l1_cmp_w1_k, l1_cmp_w2_k, l1_cmp_pe_v, l1_cmp_w1_v, l1_cmp_w2_v, l1_w_out, mem_norm_x, mem_norm_kv, mem_w_q, mem_w_k, mem_w_v, mem_q_norm, mem_k_norm, mem_w_o, ffn_norm, ffn_w_in, ffn_conv_w, ffn_conv_b, ffn_w_out):
    n_b = 2 * DN_HEADS
    c0 = 1536 + 512
    w0 = jnp.concatenate([l0_w_in[:, :c0], _pad_cols(l0_w_in[:, c0:c0 + n_b], LANES), l0_w_in[:, c0 + n_b:]], axis=1)
    c1 = (NSA_HEADS + 6 * NSA_KV_HEADS) * HEAD_DIM
    w1 = jnp.concatenate([l1_w_in[:, :c1], _pad_cols(l1_w_in[:, c1:], LANES)], axis=1)
    P = dict(
        mix_norm=mix_norm, l0_w_in=w0.astype(BF16), l0_dn_conv_w=l0_dn_conv_w, l0_dn_a_log=l0_dn_a_log,
        l0_dn_dt_bias=l0_dn_dt_bias, l0_dn_norm=l0_dn_norm, l0_swa_q_norm=l0_swa_q_norm,
        l0_swa_k_norm=l0_swa_k_norm, l0_swa_sinks=l0_swa_sinks, l0_w_out=l0_w_out.astype(BF16),
        l1_w_in=w1.astype(BF16), l1_q_norm=l1_q_norm, l1_k_norm_cmp=l1_k_norm_cmp, l1_k_norm_sel=l1_k_norm_sel,
        l1_k_norm_win=l1_k_norm_win, l1_cmp_pe_k=l1_cmp_pe_k, l1_cmp_w1_k=l1_cmp_w1_k, l1_cmp_w2_k=l1_cmp_w2_k,
        l1_cmp_pe_v=l1_cmp_pe_v, l1_cmp_w1_v=l1_cmp_w1_v, l1_cmp_w2_v=l1_cmp_w2_v,
        l1_w_out=l1_w_out.astype(BF16), mem_norm_x=mem_norm_x, mem_norm_kv=mem_norm_kv,
        mem_w_q=mem_w_q.astype(BF16), mem_w_kv=jnp.concatenate([mem_w_k, mem_w_v], axis=-1).astype(BF16),
        mem_q_norm=mem_q_norm, mem_k_norm=mem_k_norm, mem_w_o=mem_w_o.astype(BF16), ffn_norm=ffn_norm,
        ffn_w_in=ffn_w_in.astype(BF16),
        ffn_cw8=jnp.pad(ffn_conv_w.astype(F32), ((0, 0), (0, SUBLANES - ffn_conv_w.shape[1]), (0, 0))),
        ffn_cb=ffn_conv_b.astype(F32)[:, None, :], ffn_w_out=ffn_w_out.astype(BF16))

    y_p, ab_p, nsa_p, p_mem_k, p_mem_v, p_ffn = _prompt_group(x_prompt, mem_prompt, P)
    y_s, ab_s, nsa_s, s_ffn = _sample_group(
        x_sample, state_dn, state_dn_conv, cache_swa_k, cache_swa_v, cache_cmp_k, cache_cmp_v, cache_sel_k,
        cache_sel_v, cache_win_k, cache_win_v, cache_mem_k, cache_mem_v, state_ffn_conv, page_table, P)
    return (y_p, y_s, *ab_p, *nsa_p, p_mem_k, p_mem_v, p_ffn, *ab_s, *nsa_s, s_ffn)
```
